```python
import jax, jax.numpy as jnp
from jax import lax
import numpy as np

D_MODEL = 1024
BATCH = 32
SEQ = 2048
DEPTH = 1

CTX_LEN = 256
GRID_W = 64

D_MIX = D_MODEL
DA = D_MIX // 2
NA = 64
HA = DA // NA
DECAY_LORA = 64
AAA_LORA = 64
GATE_LORA = 128
RWKV_COLS = 3 * DA + 2 * DECAY_LORA + 2 * AAA_LORA + GATE_LORA
RWKV_EPS = 64e-5

DB = D_MIX - DA
HB = 4
KB = DB // HB
GDN_CONV = 5
GDN_CHUNK = 64
GDN_COLS = 4 * DB + 4 * HB
IN_COLS = RWKV_COLS + GDN_COLS

N_EXPERTS = 256
TOP_K = 8
N_GROUPS = 8
TOPK_GROUPS = 4
D_EXPERT = 256
D_SHARED = 256
ROUTED_SCALE = 2.5
MOE_BLOCK = 256
NORM_EPS = 1e-6

kernel_name = 'hybrid_rwkv7_gdn_moe_dit_layer'


def rmsnorm(x, g):
    xf = x.astype(jnp.float32)
    y = xf * lax.rsqrt(jnp.mean(xf * xf, axis=-1, keepdims=True) + NORM_EPS)
    return (y * g.astype(jnp.float32)).astype(x.dtype)


def l2norm(x):
    xf = x.astype(jnp.float32)
    return xf * lax.rsqrt(jnp.sum(xf * xf, axis=-1, keepdims=True) + 1e-6)


def shift_seq(p):
    h = p.shape[-1] // 2
    prev = jnp.pad(p[:, :-1, :h], ((0, 0), (1, 0), (0, 0)))
    nxt = jnp.pad(p[:, 1:, h:], ((0, 0), (0, 1), (0, 0)))
    return jnp.concatenate([prev, nxt], axis=-1)


def shift_grid(p):
    B, T, C = p.shape
    rows = T // GRID_W
    q = C // 4
    g = p.reshape(B, rows, GRID_W, C)
    left = jnp.pad(g[:, :, :-1, :q], ((0, 0), (0, 0), (1, 0), (0, 0)))
    right = jnp.pad(g[:, :, 1:, q:2 * q], ((0, 0), (0, 0), (0, 1), (0, 0)))
    up = jnp.pad(g[:, :-1, :, 2 * q:3 * q], ((0, 0), (1, 0), (0, 0), (0, 0)))
    down = jnp.pad(g[:, 1:, :, 3 * q:], ((0, 0), (0, 1), (0, 0), (0, 0)))
    return jnp.concatenate([left, right, up, down], axis=-1).reshape(B, T, C)


def rwkv7_prepare(p, shifted, mu, w0, w_up, a0, a_up, g_up, k_k, k_a):
    p = (p + mu * (shifted - p)).astype(jnp.float32)
    p = jnp.swapaxes(p, 0, 1)
    T, B = p.shape[:2]
    cuts = [DA, 2 * DA, 3 * DA, 3 * DA + 2 * DECAY_LORA, 3 * DA + 2 * DECAY_LORA + 2 * AAA_LORA]
    r, k, v, wl, al, gl = jnp.split(p, cuts, axis=-1)
    wl = wl.reshape(T, B, 2, DECAY_LORA)
    al = al.reshape(T, B, 2, AAA_LORA)
    w_log = -jax.nn.softplus(-(w0 + jnp.einsum('tbdl,dlc->tbdc', jnp.tanh(wl), w_up))) - 0.5
    decay = jnp.exp(-jnp.exp(w_log))
    a = jax.nn.sigmoid(a0 + jnp.einsum('tbdl,dlc->tbdc', al, a_up))
    g = jax.nn.sigmoid(gl) @ g_up
    kk = l2norm((k * k_k).reshape(T, B, HA, NA))
    k_dir = k[:, :, None] * (1.0 + (a - 1.0) * k_a)
    b_dir = kk.reshape(T, B, 1, DA) * a
    heads = lambda t: t.reshape(t.shape[:-1] + (HA, NA))
    return heads(r), heads(v), kk, heads(decay), heads(k_dir), heads(b_dir), g


def wkv7_scan(r, decay, k, v, kk, b, s0, reverse):
    def step(S, inp):
        r_t, w_t, k_t, v_t, kk_t, b_t = inp
        sa = -jnp.einsum('bhvk,bhk->bhv', S, kk_t)
        S = S * w_t[:, :, None, :] + sa[..., None] * b_t[:, :, None, :] + v_t[..., None] * k_t[:, :, None, :]
        return S, jnp.einsum('bhvk,bhk->bhv', S, r_t)
    return lax.scan(step, s0, (r, decay, k, v, kk, b), reverse=reverse)


def rwkv7_mixer(p_ctx, p_lat, mu, w0, w_up, a0, a_up, g_up, k_k, k_a, r_k, lnx_w, lnx_b):
    args = (mu, w0, w_up, a0, a_up, g_up, k_k, k_a)
    ctx_t = rwkv7_prepare(p_ctx, shift_seq(p_ctx), *args)
    lat_t = rwkv7_prepare(p_lat, shift_grid(p_lat), *args)
    zero = jnp.zeros((p_lat.shape[0], HA, NA, NA), jnp.float32)

    def run(t, s_f, s_b):
        r, v, kk, decay, k_dir, b_dir, g = t
        T, B = r.shape[:2]
        s_f, y_f = wkv7_scan(r, decay[:, :, 0], k_dir[:, :, 0], v, kk, b_dir[:, :, 0], s_f, False)
        s_b, y_b = wkv7_scan(r, decay[:, :, 1], k_dir[:, :, 1], v, kk, b_dir[:, :, 1], s_b, True)
        y = y_f + y_b
        m = jnp.mean(y, axis=-1, keepdims=True)
        var = jnp.mean(jnp.square(y - m), axis=-1, keepdims=True)
        y = ((y - m) * lax.rsqrt(var + RWKV_EPS)).reshape(T, B, DA) * lnx_w + lnx_b
        bonus = jnp.sum(r[:, :, None] * k_dir * r_k, axis=(2, 4))
        y = (y + (bonus[..., None] * v).reshape(T, B, DA)) * g
        return jnp.swapaxes(y, 0, 1), s_f, s_b

    y_ctx, s_f, s_b = run(ctx_t, zero, zero)
    y_lat, _, _ = run(lat_t, s_f, s_b)
    return y_ctx, y_lat


def dwconv_centered(x, w):
    C = x.shape[-1]
    return lax.conv_general_dilated(
        x, w[:, None, :].astype(x.dtype), window_strides=(1,),
        padding=[(GDN_CONV // 2, GDN_CONV // 2)],
        dimension_numbers=('NWC', 'WIO', 'NWC'), feature_group_count=C)


def gdn_prepare(p, conv_w, A_log, dt_bias):
    B, T, _ = p.shape
    qkv = jax.nn.silu(dwconv_centered(p[..., :3 * DB], conv_w)).astype(jnp.float32)
    q, k, v = jnp.split(qkv, 3, axis=-1)
    q = l2norm(q.reshape(B, T, HB, KB)) * (KB ** -0.5)
    k = l2norm(k.reshape(B, T, HB, KB))
    v = v.reshape(B, T, HB, KB)
    z = p[..., 3 * DB:4 * DB]
    gl = p[..., 4 * DB:].astype(jnp.float32).reshape(B, T, 2, 2, HB)
    g = -jnp.exp(A_log) * jax.nn.softplus(gl[:, :, 0] + dt_bias)
    beta = jax.nn.sigmoid(gl[:, :, 1])
    return q, k, v, g, beta, z


def gdn_chunked(q, k, v, g, beta, s0):
    B, T, H, K = q.shape
    V = v.shape[-1]
    C = GDN_CHUNK
    n = T // C
    blk = lambda t: jnp.moveaxis(t.reshape((B, n, C, H) + t.shape[3:]), 3, 1)
    q, k, v, g, beta = (blk(t) for t in (q, k, v, g, beta))
    gc = jnp.cumsum(g, axis=-1)
    idx = jnp.arange(C)
    incl = idx[:, None] >= idx[None, :]
    strict = idx[:, None] > idx[None, :]
    decay = jnp.where(incl, jnp.exp(jnp.where(incl, gc[..., :, None] - gc[..., None, :], 0.0)), 0.0)
    kb = k * beta[..., None]
    a_low = jnp.where(strict, jnp.einsum('bhnik,bhnjk->bhnij', kb, k) * decay, 0.0)
    rhs = jnp.concatenate([v * beta[..., None], kb * jnp.exp(gc)[..., None]], axis=-1)
    sol = lax.linalg.triangular_solve(a_low + jnp.eye(C, dtype=jnp.float32), rhs,
                                      left_side=True, lower=True, unit_diagonal=True)
    u, w = sol[..., :V], sol[..., V:]
    attn = jnp.einsum('bhnik,bhnjk->bhnij', q, k) * decay
    qg = q * jnp.exp(gc)[..., None]
    kg = k * jnp.exp(gc[..., -1:] - gc)[..., None]
    g_last = jnp.exp(gc[..., -1])

    def step(S, inp):
        u_i, w_i, a_i, qg_i, kg_i, gl_i = inp
        v_new = u_i - jnp.einsum('bhck,bhkv->bhcv', w_i, S)
        o = jnp.einsum('bhck,bhkv->bhcv', qg_i, S) + jnp.einsum('bhcs,bhsv->bhcv', a_i, v_new)
        S = S * gl_i[..., None, None] + jnp.einsum('bhck,bhcv->bhkv', kg_i, v_new)
        return S, o

    xs = tuple(jnp.moveaxis(t, 2, 0) for t in (u, w, attn, qg, kg, g_last))
    S, o = lax.scan(step, s0, xs)
    o = jnp.transpose(o, (1, 0, 3, 2, 4)).reshape(B, T, H, V)
    return S, o


def gdn_mixer(p_ctx, p_lat, conv_w, A_log, dt_bias, onorm_g):
    ctx_t = gdn_prepare(p_ctx, conv_w, A_log, dt_bias)
    lat_t = gdn_prepare(p_lat, conv_w, A_log, dt_bias)
    zero = jnp.zeros((p_lat.shape[0], HB, KB, KB), jnp.float32)
    flip = lambda t: jnp.flip(t, axis=1)

    def run(t, s_f, s_b):
        q, k, v, g, beta, z = t
        B, T = q.shape[:2]
        s_f, o_f = gdn_chunked(q, k, v, g[:, :, 0], beta[:, :, 0], s_f)
        s_b, o_b = gdn_chunked(flip(q), flip(k), flip(v), flip(g[:, :, 1]), flip(beta[:, :, 1]), s_b)
        o = o_f + flip(o_b)
        o = o * lax.rsqrt(jnp.mean(o * o, axis=-1, keepdims=True) + NORM_EPS) * onorm_g
        return o.reshape(B, T, DB) * jax.nn.silu(z.astype(jnp.float32)), s_f, s_b

    y_ctx, s_f, s_b = run(ctx_t, zero, zero)
    y_lat, _, _ = run(lat_t, s_f, s_b)
    return y_ctx, y_lat


def moe_ffn(h, router_w, router_b, w1, w3, w2, sw1, sw3, sw2):
    B, T, D = h.shape
    N = B * T
    xs = h.reshape(N, D)
    scores = jax.nn.sigmoid((xs @ router_w).astype(jnp.float32))
    sel = scores + router_b.astype(jnp.float32)
    grp_score = jnp.sum(lax.top_k(sel.reshape(N, N_GROUPS, N_EXPERTS // N_GROUPS), 2)[0], axis=-1)
    top_g = lax.top_k(grp_score, TOPK_GROUPS)[1]
    gmask = jnp.any(top_g[:, :, None] == jnp.arange(N_GROUPS)[None, None, :], axis=1)
    emask = jnp.repeat(gmask, N_EXPERTS // N_GROUPS, axis=1)
    top_e = lax.top_k(jnp.where(emask, sel, -jnp.inf), TOP_K)[1]
    wts = jnp.take_along_axis(scores, top_e, axis=1)
    wts = wts / jnp.sum(wts, axis=-1, keepdims=True) * ROUTED_SCALE

    nk = N * TOP_K
    e_flat = top_e.reshape(nk)
    order = jnp.argsort(e_flat)
    e_sorted = e_flat[order]
    tok_sorted = (jnp.arange(nk, dtype=jnp.int32) // TOP_K)[order]
    w_sorted = wts.reshape(nk)[order]
    cnt = jnp.bincount(e_flat, length=N_EXPERTS)
    padded = (cnt + MOE_BLOCK - 1) // MOE_BLOCK * MOE_BLOCK
    pend = jnp.cumsum(padded)
    dest = (pend - padded)[e_sorted] + jnp.arange(nk, dtype=jnp.int32) - (jnp.cumsum(cnt) - cnt)[e_sorted]
    n_blk = (nk + N_EXPERTS * (MOE_BLOCK - 1) + MOE_BLOCK - 1) // MOE_BLOCK
    buf_tok = jnp.full((n_blk * MOE_BLOCK,), N, jnp.int32).at[dest].set(tok_sorted)
    buf_w = jnp.zeros((n_blk * MOE_BLOCK,), h.dtype).at[dest].set(w_sorted.astype(h.dtype))
    blk_e = jnp.minimum(jnp.searchsorted(pend, jnp.arange(n_blk, dtype=jnp.int32) * MOE_BLOCK, side='right'),
                        N_EXPERTS - 1)
    x_pad = jnp.concatenate([xs, jnp.zeros((1, D), xs.dtype)], axis=0)

    def block(acc, inp):
        tok, wt, e = inp
        xb = x_pad[tok]
        yb = (jax.nn.silu(xb @ w1[e]) * (xb @ w3[e])) @ w2[e]
        return acc.at[tok].add(yb * wt[:, None]), None

    acc, _ = lax.scan(block, jnp.zeros((N + 1, D), h.dtype),
                      (buf_tok.reshape(n_blk, MOE_BLOCK), buf_w.reshape(n_blk, MOE_BLOCK), blk_e))
    shared = (jax.nn.silu(xs @ sw1) * (xs @ sw3)) @ sw2
    return (acc[:N] + shared).reshape(B, T, D)


def setup_inputs(seed: int = 0) -> dict:
    key = jax.random.key(seed)
    ks = iter(jax.random.split(key, 48))
    nrm = lambda shape, s: jax.random.normal(next(ks), shape, jnp.float32) * s
    unif = lambda shape, lo, hi: jax.random.uniform(next(ks), shape, jnp.float32, lo, hi)
    L, D = DEPTH, D_MODEL
    dt = unif((L, 2, HB), 1e-3, 1e-1)
    return {
        'x': nrm((BATCH, SEQ, D), 1.0),
        'c': nrm((BATCH, D), 1.0),
        'ctx': nrm((BATCH, CTX_LEN, D), 1.0),
        'c_ctx': nrm((D,), 1.0),
        'w_ada': nrm((L, D, 6 * D), 0.5 * D ** -0.5),
        'b_ada': nrm((L, 6 * D), 0.02),
        'norm1_g': 1.0 + nrm((L, D), 0.02),
        'w_in': nrm((L, D, IN_COLS), D ** -0.5),
        'mu_shift': unif((L, RWKV_COLS), 0.0, 1.0),
        'w0': unif((L, 2, DA), -6.0, -1.0),
        'w_up': nrm((L, 2, DECAY_LORA, DA), 0.1),
        'a0': nrm((L, 2, DA), 0.1),
        'a_up': nrm((L, 2, AAA_LORA, DA), 0.1),
        'g_up': nrm((L, GATE_LORA, DA), GATE_LORA ** -0.5),
        'k_k': 0.85 + nrm((L, DA), 0.05),
        'k_a': 1.0 + nrm((L, DA), 0.05),
        'r_k': nrm((L, HA, NA), 0.1),
        'lnx_w': 1.0 + nrm((L, DA), 0.02),
        'lnx_b': nrm((L, DA), 0.02),
        'conv_w': nrm((L, GDN_CONV, 3 * DB), GDN_CONV ** -0.5),
        'A_log': jnp.log(unif((L, 2, HB), 1.0, 16.0)),
        'dt_bias': dt + jnp.log(-jnp.expm1(-dt)),
        'onorm_g': 1.0 + nrm((L, KB), 0.02),
        'w_out': nrm((L, D_MIX, D), D_MIX ** -0.5),
        'norm2_g': 1.0 + nrm((L, D), 0.02),
        'router_w': nrm((L, D, N_EXPERTS), D ** -0.5),
        'router_b': nrm((L, N_EXPERTS), 0.01),
        'exp_w1': nrm((L, N_EXPERTS, D, D_EXPERT), D ** -0.5),
        'exp_w3': nrm((L, N_EXPERTS, D, D_EXPERT), D ** -0.5),
        'exp_w2': nrm((L, N_EXPERTS, D_EXPERT, D), D_EXPERT ** -0.5),
        'sh_w1': nrm((L, D, D_SHARED), D ** -0.5),
        'sh_w3': nrm((L, D, D_SHARED), D ** -0.5),
        'sh_w2': nrm((L, D_SHARED, D), D_SHARED ** -0.5),
        'final_g': 1.0 + nrm((D,), 0.02),
    }


def reference(x, c, ctx, c_ctx, w_ada, b_ada, norm1_g, w_in, mu_shift, w0, w_up, a0, a_up, g_up,
              k_k, k_a, r_k, lnx_w, lnx_b, conv_w, A_log, dt_bias, onorm_g, w_out, norm2_g,
              router_w, router_b, exp_w1, exp_w3, exp_w2, sh_w1, sh_w3, sh_w2, final_g):
    h_x, h_c = x, ctx
    for l in range(DEPTH):
        mod = jax.nn.silu(c) @ w_ada[l] + b_ada[l]
        mod_c = jax.nn.silu(c_ctx) @ w_ada[l] + b_ada[l]
        sh1, sc1, gt1, sh2, sc2, gt2 = (m[:, None] for m in jnp.split(mod, 6, axis=-1))
        sh1c, sc1c, gt1c, sh2c, sc2c, gt2c = jnp.split(mod_c, 6, axis=-1)

        p_x = (rmsnorm(h_x, norm1_g[l]) * (1.0 + sc1) + sh1) @ w_in[l]
        p_c = (rmsnorm(h_c, norm1_g[l]) * (1.0 + sc1c) + sh1c) @ w_in[l]
        ya_c, ya_x = rwkv7_mixer(p_c[..., :RWKV_COLS], p_x[..., :RWKV_COLS], mu_shift[l], w0[l], w_up[l],
                                 a0[l], a_up[l], g_up[l], k_k[l], k_a[l], r_k[l], lnx_w[l], lnx_b[l])
        yb_c, yb_x = gdn_mixer(p_c[..., RWKV_COLS:], p_x[..., RWKV_COLS:], conv_w[l], A_log[l],
                               dt_bias[l], onorm_g[l])
        h_x = h_x + gt1 * (jnp.concatenate([ya_x, yb_x], axis=-1).astype(x.dtype) @ w_out[l])

        h_x = h_x + gt2 * moe_ffn(rmsnorm(h_x, norm2_g[l]) * (1.0 + sc2) + sh2, router_w[l], router_b[l],
                                  exp_w1[l], exp_w3[l], exp_w2[l], sh_w1[l], sh_w3[l], sh_w2[l])

        if l < DEPTH - 1:
            h_c = h_c + gt1c * (jnp.concatenate([ya_c, yb_c], axis=-1).astype(x.dtype) @ w_out[l])
            h_c = h_c + gt2c * moe_ffn(rmsnorm(h_c, norm2_g[l]) * (1.0 + sc2c) + sh2c, router_w[l], router_b[l],
                                       exp_w1[l], exp_w3[l], exp_w2[l], sh_w1[l], sh_w3[l], sh_w2[l])
    return rmsnorm(h_x, final_g)
```

```python
import functools

import jax
import jax.numpy as jnp
from jax import lax
from jax.experimental import pallas as pl
from jax.experimental.pallas import tpu as pltpu

D_MODEL = 1024
BATCH = 32
SEQ = 2048
DEPTH = 1
CTX_LEN = 256
GRID_W = 64
D_MIX = D_MODEL
DA = D_MIX // 2
NA = 64
HA = DA // NA
DECAY_LORA = 64
AAA_LORA = 64
GATE_LORA = 128
RWKV_COLS = 3 * DA + 2 * DECAY_LORA + 2 * AAA_LORA + GATE_LORA
RWKV_EPS = 64e-5
DB = D_MIX - DA
HB = 4
KB = DB // HB
GDN_CONV = 5
GDN_CHUNK = 64
GDN_COLS = 4 * DB + 4 * HB
IN_COLS = RWKV_COLS + GDN_COLS
N_EXPERTS = 256
TOP_K = 8
N_GROUPS = 8
TOPK_GROUPS = 4
D_EXPERT = 256
D_SHARED = 256
ROUTED_SCALE = 2.5
MOE_BLOCK = 256
NORM_EPS = 1e-6


def rmsnorm(x, g):
    xf = x.astype(jnp.float32)
    y = xf * lax.rsqrt(jnp.mean(xf * xf, axis=-1, keepdims=True) + NORM_EPS)
    return (y * g.astype(jnp.float32)).astype(x.dtype)


def l2norm(x):
    xf = x.astype(jnp.float32)
    return xf * lax.rsqrt(jnp.sum(xf * xf, axis=-1, keepdims=True) + 1e-6)


def shift_seq(p):
    h = p.shape[-1] // 2
    prev = jnp.pad(p[:, :-1, :h], ((0, 0), (1, 0), (0, 0)))
    nxt = jnp.pad(p[:, 1:, h:], ((0, 0), (0, 1), (0, 0)))
    return jnp.concatenate([prev, nxt], axis=-1)


def shift_grid(p):
    B, T, C = p.shape
    rows = T // GRID_W
    q = C // 4
    g = p.reshape(B, rows, GRID_W, C)
    left = jnp.pad(g[:, :, :-1, :q], ((0, 0), (0, 0), (1, 0), (0, 0)))
    right = jnp.pad(g[:, :, 1:, q:2 * q], ((0, 0), (0, 0), (0, 1), (0, 0)))
    up = jnp.pad(g[:, :-1, :, 2 * q:3 * q], ((0, 0), (1, 0), (0, 0), (0, 0)))
    down = jnp.pad(g[:, 1:, :, 3 * q:], ((0, 0), (0, 1), (0, 0), (0, 0)))
    return jnp.concatenate([left, right, up, down], axis=-1).reshape(B, T, C)


def rwkv7_prepare(p, shifted, mu, w0, w_up, a0, a_up, g_up, k_k, k_a):
    p = (p + mu * (shifted - p)).astype(jnp.float32)
    p = jnp.swapaxes(p, 0, 1)
    T, B = p.shape[:2]
    cuts = [DA, 2 * DA, 3 * DA, 3 * DA + 2 * DECAY_LORA, 3 * DA + 2 * DECAY_LORA + 2 * AAA_LORA]
    r, k, v, wl, al, gl = jnp.split(p, cuts, axis=-1)
    wl = wl.reshape(T, B, 2, DECAY_LORA)
    al = al.reshape(T, B, 2, AAA_LORA)
    w_log = -jax.nn.softplus(-(w0 + jnp.einsum('tbdl,dlc->tbdc', jnp.tanh(wl), w_up))) - 0.5
    decay = jnp.exp(-jnp.exp(w_log))
    a = jax.nn.sigmoid(a0 + jnp.einsum('tbdl,dlc->tbdc', al, a_up))
    g = jax.nn.sigmoid(gl) @ g_up
    kk = l2norm((k * k_k).reshape(T, B, HA, NA))
    k_dir = k[:, :, None] * (1.0 + (a - 1.0) * k_a)
    b_dir = kk.reshape(T, B, 1, DA) * a
    heads = lambda t: t.reshape(t.shape[:-1] + (HA, NA))
    return heads(r), heads(v), kk, heads(decay), heads(k_dir), heads(b_dir), g


def wkv7_scan(r, decay, k, v, kk, b, s0, reverse):
    def step(S, inp):
        r_t, w_t, k_t, v_t, kk_t, b_t = inp
        sa = -jnp.einsum('bhvk,bhk->bhv', S, kk_t)
        S = S * w_t[:, :, None, :] + sa[..., None] * b_t[:, :, None, :] + v_t[..., None] * k_t[:, :, None, :]
        return S, jnp.einsum('bhvk,bhk->bhv', S, r_t)
    return lax.scan(step, s0, (r, decay, k, v, kk, b), reverse=reverse)


def rwkv7_mixer(p_ctx, p_lat, mu, w0, w_up, a0, a_up, g_up, k_k, k_a, r_k, lnx_w, lnx_b):
    args = (mu, w0, w_up, a0, a_up, g_up, k_k, k_a)
    ctx_t = rwkv7_prepare(p_ctx, shift_seq(p_ctx), *args)
    lat_t = rwkv7_prepare(p_lat, shift_grid(p_lat), *args)
    zero = jnp.zeros((p_lat.shape[0], HA, NA, NA), jnp.float32)

    def run(t, s_f, s_b):
        r, v, kk, decay, k_dir, b_dir, g = t
        T, B = r.shape[:2]
        s_f, y_f = wkv7_scan(r, decay[:, :, 0], k_dir[:, :, 0], v, kk, b_dir[:, :, 0], s_f, False)
        s_b, y_b = wkv7_scan(r, decay[:, :, 1], k_dir[:, :, 1], v, kk, b_dir[:, :, 1], s_b, True)
        y = y_f + y_b
        m = jnp.mean(y, axis=-1, keepdims=True)
        var = jnp.mean(jnp.square(y - m), axis=-1, keepdims=True)
        y = ((y - m) * lax.rsqrt(var + RWKV_EPS)).reshape(T, B, DA) * lnx_w + lnx_b
        bonus = jnp.sum(r[:, :, None] * k_dir * r_k, axis=(2, 4))
        y = (y + (bonus[..., None] * v).reshape(T, B, DA)) * g
        return jnp.swapaxes(y, 0, 1), s_f, s_b

    y_ctx, s_f, s_b = run(ctx_t, zero, zero)
    y_lat, _, _ = run(lat_t, s_f, s_b)
    return y_ctx, y_lat


def dwconv_centered(x, w):
    C = x.shape[-1]
    return lax.conv_general_dilated(
        x, w[:, None, :].astype(x.dtype), window_strides=(1,),
        padding=[(GDN_CONV // 2, GDN_CONV // 2)],
        dimension_numbers=('NWC', 'WIO', 'NWC'), feature_group_count=C)


def gdn_prepare(p, conv_w, A_log, dt_bias):
    B, T, _ = p.shape
    qkv = jax.nn.silu(dwconv_centered(p[..., :3 * DB], conv_w)).astype(jnp.float32)
    q, k, v = jnp.split(qkv, 3, axis=-1)
    q = l2norm(q.reshape(B, T, HB, KB)) * (KB ** -0.5)
    k = l2norm(k.reshape(B, T, HB, KB))
    v = v.reshape(B, T, HB, KB)
    z = p[..., 3 * DB:4 * DB]
    gl = p[..., 4 * DB:].astype(jnp.float32).reshape(B, T, 2, 2, HB)
    g = -jnp.exp(A_log) * jax.nn.softplus(gl[:, :, 0] + dt_bias)
    beta = jax.nn.sigmoid(gl[:, :, 1])
    return q, k, v, g, beta, z


def gdn_chunked(q, k, v, g, beta, s0):
    B, T, H, K = q.shape
    V = v.shape[-1]
    C = GDN_CHUNK
    n = T // C
    blk = lambda t: jnp.moveaxis(t.reshape((B, n, C, H) + t.shape[3:]), 3, 1)
    q, k, v, g, beta = (blk(t) for t in (q, k, v, g, beta))
    gc = jnp.cumsum(g, axis=-1)
    idx = jnp.arange(C)
    incl = idx[:, None] >= idx[None, :]
    strict = idx[:, None] > idx[None, :]
    decay = jnp.where(incl, jnp.exp(jnp.where(incl, gc[..., :, None] - gc[..., None, :], 0.0)), 0.0)
    kb = k * beta[..., None]
    a_low = jnp.where(strict, jnp.einsum('bhnik,bhnjk->bhnij', kb, k) * decay, 0.0)
    rhs = jnp.concatenate([v * beta[..., None], kb * jnp.exp(gc)[..., None]], axis=-1)
    sol = lax.linalg.triangular_solve(a_low + jnp.eye(C, dtype=jnp.float32), rhs,
                                      left_side=True, lower=True, unit_diagonal=True)
    u, w = sol[..., :V], sol[..., V:]
    attn = jnp.einsum('bhnik,bhnjk->bhnij', q, k) * decay
    qg = q * jnp.exp(gc)[..., None]
    kg = k * jnp.exp(gc[..., -1:] - gc)[..., None]
    g_last = jnp.exp(gc[..., -1])

    def step(S, inp):
        u_i, w_i, a_i, qg_i, kg_i, gl_i = inp
        v_new = u_i - jnp.einsum('bhck,bhkv->bhcv', w_i, S)
        o = jnp.einsum('bhck,bhkv->bhcv', qg_i, S) + jnp.einsum('bhcs,bhsv->bhcv', a_i, v_new)
        S = S * gl_i[..., None, None] + jnp.einsum('bhck,bhcv->bhkv', kg_i, v_new)
        return S, o

    xs = tuple(jnp.moveaxis(t, 2, 0) for t in (u, w, attn, qg, kg, g_last))
    S, o = lax.scan(step, s0, xs)
    o = jnp.transpose(o, (1, 0, 3, 2, 4)).reshape(B, T, H, V)
    return S, o


def gdn_mixer(p_ctx, p_lat, conv_w, A_log, dt_bias, onorm_g):
    ctx_t = gdn_prepare(p_ctx, conv_w, A_log, dt_bias)
    lat_t = gdn_prepare(p_lat, conv_w, A_log, dt_bias)
    zero = jnp.zeros((p_lat.shape[0], HB, KB, KB), jnp.float32)
    flip = lambda t: jnp.flip(t, axis=1)

    def run(t, s_f, s_b):
        q, k, v, g, beta, z = t
        B, T = q.shape[:2]
        s_f, o_f = gdn_chunked(q, k, v, g[:, :, 0], beta[:, :, 0], s_f)
        s_b, o_b = gdn_chunked(flip(q), flip(k), flip(v), flip(g[:, :, 1]), flip(beta[:, :, 1]), s_b)
        o = o_f + flip(o_b)
        o = o * lax.rsqrt(jnp.mean(o * o, axis=-1, keepdims=True) + NORM_EPS) * onorm_g
        return o.reshape(B, T, DB) * jax.nn.silu(z.astype(jnp.float32)), s_f, s_b

    y_ctx, s_f, s_b = run(ctx_t, zero, zero)
    y_lat, _, _ = run(lat_t, s_f, s_b)
    return y_ctx, y_lat


def moe_ffn(h, router_w, router_b, w1, w3, w2, sw1, sw3, sw2):
    B, T, D = h.shape
    N = B * T
    xs = h.reshape(N, D)
    scores = jax.nn.sigmoid((xs @ router_w).astype(jnp.float32))
    sel = scores + router_b.astype(jnp.float32)
    grp_score = jnp.sum(lax.top_k(sel.reshape(N, N_GROUPS, N_EXPERTS // N_GROUPS), 2)[0], axis=-1)
    top_g = lax.top_k(grp_score, TOPK_GROUPS)[1]
    gmask = jnp.any(top_g[:, :, None] == jnp.arange(N_GROUPS)[None, None, :], axis=1)
    emask = jnp.repeat(gmask, N_EXPERTS // N_GROUPS, axis=1)
    top_e = lax.top_k(jnp.where(emask, sel, -jnp.inf), TOP_K)[1]
    wts = jnp.take_along_axis(scores, top_e, axis=1)
    wts = wts / jnp.sum(wts, axis=-1, keepdims=True) * ROUTED_SCALE

    nk = N * TOP_K
    e_flat = top_e.reshape(nk)
    order = jnp.argsort(e_flat)
    e_sorted = e_flat[order]
    tok_sorted = (jnp.arange(nk, dtype=jnp.int32) // TOP_K)[order]
    w_sorted = wts.reshape(nk)[order]
    cnt = jnp.bincount(e_flat, length=N_EXPERTS)
    padded = (cnt + MOE_BLOCK - 1) // MOE_BLOCK * MOE_BLOCK
    pend = jnp.cumsum(padded)
    dest = (pend - padded)[e_sorted] + jnp.arange(nk, dtype=jnp.int32) - (jnp.cumsum(cnt) - cnt)[e_sorted]
    n_blk = (nk + N_EXPERTS * (MOE_BLOCK - 1) + MOE_BLOCK - 1) // MOE_BLOCK
    buf_tok = jnp.full((n_blk * MOE_BLOCK,), N, jnp.int32).at[dest].set(tok_sorted)
    buf_w = jnp.zeros((n_blk * MOE_BLOCK,), h.dtype).at[dest].set(w_sorted.astype(h.dtype))
    blk_e = jnp.minimum(jnp.searchsorted(pend, jnp.arange(n_blk, dtype=jnp.int32) * MOE_BLOCK, side='right'),
                        N_EXPERTS - 1)
    x_pad = jnp.concatenate([xs, jnp.zeros((1, D), xs.dtype)], axis=0)

    def block(acc, inp):
        tok, wt, e = inp
        xb = x_pad[tok]
        yb = (jax.nn.silu(xb @ w1[e]) * (xb @ w3[e])) @ w2[e]
        return acc.at[tok].add(yb * wt[:, None]), None

    acc, _ = lax.scan(block, jnp.zeros((N + 1, D), h.dtype),
                      (buf_tok.reshape(n_blk, MOE_BLOCK), buf_w.reshape(n_blk, MOE_BLOCK), blk_e))
    shared = (jax.nn.silu(xs @ sw1) * (xs @ sw3)) @ sw2
    return (acc[:N] + shared).reshape(B, T, D)


def _final_norm_body(h_ref, g_ref, o_ref):
    h = h_ref[...]
    ms = jnp.mean(h * h, axis=-1, keepdims=True)
    o_ref[...] = h * lax.rsqrt(ms + NORM_EPS) * g_ref[...]


def final_norm(h, g):
    n, d = h.shape
    rows = 1024
    return pl.pallas_call(
        _final_norm_body,
        grid=(n // rows,),
        in_specs=[pl.BlockSpec((rows, d), lambda i: (i, 0)), pl.BlockSpec((1, d), lambda i: (0, 0))],
        out_specs=pl.BlockSpec((rows, d), lambda i: (i, 0)),
        out_shape=jax.ShapeDtypeStruct((n, d), h.dtype),
    )(h, g.reshape(1, d))


def kernel(x, c, ctx, c_ctx, w_ada, b_ada, norm1_g, w_in, mu_shift, w0, w_up, a0, a_up, g_up,
           k_k, k_a, r_k, lnx_w, lnx_b, conv_w, A_log, dt_bias, onorm_g, w_out, norm2_g,
           router_w, router_b, exp_w1, exp_w3, exp_w2, sh_w1, sh_w3, sh_w2, final_g):
    h_x, h_c = x, ctx
    l = 0
    mod = jax.nn.silu(c) @ w_ada[l] + b_ada[l]
    mod_c = jax.nn.silu(c_ctx) @ w_ada[l] + b_ada[l]
    sh1, sc1, gt1, sh2, sc2, gt2 = (m[:, None] for m in jnp.split(mod, 6, axis=-1))
    sh1c, sc1c, gt1c, sh2c, sc2c, gt2c = jnp.split(mod_c, 6, axis=-1)

    p_x = (rmsnorm(h_x, norm1_g[l]) * (1.0 + sc1) + sh1) @ w_in[l]
    p_c = (rmsnorm(h_c, norm1_g[l]) * (1.0 + sc1c) + sh1c) @ w_in[l]
    ya_c, ya_x = rwkv7_mixer(p_c[..., :RWKV_COLS], p_x[..., :RWKV_COLS], mu_shift[l], w0[l], w_up[l],
                             a0[l], a_up[l], g_up[l], k_k[l], k_a[l], r_k[l], lnx_w[l], lnx_b[l])
    yb_c, yb_x = gdn_mixer(p_c[..., RWKV_COLS:], p_x[..., RWKV_COLS:], conv_w[l], A_log[l],
                           dt_bias[l], onorm_g[l])
    h_x = h_x + gt1 * (jnp.concatenate([ya_x, yb_x], axis=-1).astype(x.dtype) @ w_out[l])
    h_x = h_x + gt2 * moe_ffn(rmsnorm(h_x, norm2_g[l]) * (1.0 + sc2) + sh2, router_w[l], router_b[l],
                              exp_w1[l], exp_w3[l], exp_w2[l], sh_w1[l], sh_w3[l], sh_w2[l])
    return final_norm(h_x.reshape(BATCH * SEQ, D_MODEL), final_g).reshape(BATCH, SEQ, D_MODEL)
```

```python
import functools

import jax
import jax.numpy as jnp
from jax import lax
from jax.experimental import pallas as pl
from jax.experimental.pallas import tpu as pltpu

D_MODEL = 1024
BATCH = 32
SEQ = 2048
DEPTH = 1
CTX_LEN = 256
GRID_W = 64
D_MIX = D_MODEL
DA = D_MIX // 2
NA = 64
HA = DA // NA
DECAY_LORA = 64
AAA_LORA = 64
GATE_LORA = 128
RWKV_COLS = 3 * DA + 2 * DECAY_LORA + 2 * AAA_LORA + GATE_LORA
RWKV_EPS = 64e-5
DB = D_MIX - DA
HB = 4
KB = DB // HB
GDN_CONV = 5
GDN_CHUNK = 64
GDN_COLS = 4 * DB + 4 * HB
IN_COLS = RWKV_COLS + GDN_COLS
N_EXPERTS = 256
TOP_K = 8
N_GROUPS = 8
TOPK_GROUPS = 4
D_EXPERT = 256
D_SHARED = 256
ROUTED_SCALE = 2.5
MOE_BLOCK = 256
NORM_EPS = 1e-6


def rmsnorm(x, g):
    xf = x.astype(jnp.float32)
    y = xf * lax.rsqrt(jnp.mean(xf * xf, axis=-1, keepdims=True) + NORM_EPS)
    return (y * g.astype(jnp.float32)).astype(x.dtype)


def l2norm(x):
    xf = x.astype(jnp.float32)
    return xf * lax.rsqrt(jnp.sum(xf * xf, axis=-1, keepdims=True) + 1e-6)


def shift_seq(p):
    h = p.shape[-1] // 2
    prev = jnp.pad(p[:, :-1, :h], ((0, 0), (1, 0), (0, 0)))
    nxt = jnp.pad(p[:, 1:, h:], ((0, 0), (0, 1), (0, 0)))
    return jnp.concatenate([prev, nxt], axis=-1)


def shift_grid(p):
    B, T, C = p.shape
    rows = T // GRID_W
    q = C // 4
    g = p.reshape(B, rows, GRID_W, C)
    left = jnp.pad(g[:, :, :-1, :q], ((0, 0), (0, 0), (1, 0), (0, 0)))
    right = jnp.pad(g[:, :, 1:, q:2 * q], ((0, 0), (0, 0), (0, 1), (0, 0)))
    up = jnp.pad(g[:, :-1, :, 2 * q:3 * q], ((0, 0), (1, 0), (0, 0), (0, 0)))
    down = jnp.pad(g[:, 1:, :, 3 * q:], ((0, 0), (0, 1), (0, 0), (0, 0)))
    return jnp.concatenate([left, right, up, down], axis=-1).reshape(B, T, C)


def dwconv_centered(x, w):
    C = x.shape[-1]
    return lax.conv_general_dilated(
        x, w[:, None, :].astype(x.dtype), window_strides=(1,),
        padding=[(GDN_CONV // 2, GDN_CONV // 2)],
        dimension_numbers=('NWC', 'WIO', 'NWC'), feature_group_count=C)


def gdn_prepare(p, conv_w, A_log, dt_bias):
    B, T, _ = p.shape
    qkv = jax.nn.silu(dwconv_centered(p[..., :3 * DB], conv_w)).astype(jnp.float32)
    q, k, v = jnp.split(qkv, 3, axis=-1)
    q = l2norm(q.reshape(B, T, HB, KB)) * (KB ** -0.5)
    k = l2norm(k.reshape(B, T, HB, KB))
    v = v.reshape(B, T, HB, KB)
    z = p[..., 3 * DB:4 * DB]
    gl = p[..., 4 * DB:].astype(jnp.float32).reshape(B, T, 2, 2, HB)
    g = -jnp.exp(A_log) * jax.nn.softplus(gl[:, :, 0] + dt_bias)
    beta = jax.nn.sigmoid(gl[:, :, 1])
    return q, k, v, g, beta, z


def gdn_chunked(q, k, v, g, beta, s0):
    B, T, H, K = q.shape
    V = v.shape[-1]
    C = GDN_CHUNK
    n = T // C
    blk = lambda t: jnp.moveaxis(t.reshape((B, n, C, H) + t.shape[3:]), 3, 1)
    q, k, v, g, beta = (blk(t) for t in (q, k, v, g, beta))
    gc = jnp.cumsum(g, axis=-1)
    idx = jnp.arange(C)
    incl = idx[:, None] >= idx[None, :]
    strict = idx[:, None] > idx[None, :]
    decay = jnp.where(incl, jnp.exp(jnp.where(incl, gc[..., :, None] - gc[..., None, :], 0.0)), 0.0)
    kb = k * beta[..., None]
    a_low = jnp.where(strict, jnp.einsum('bhnik,bhnjk->bhnij', kb, k) * decay, 0.0)
    rhs = jnp.concatenate([v * beta[..., None], kb * jnp.exp(gc)[..., None]], axis=-1)
    sol = lax.linalg.triangular_solve(a_low + jnp.eye(C, dtype=jnp.float32), rhs,
                                      left_side=True, lower=True, unit_diagonal=True)
    u, w = sol[..., :V], sol[..., V:]
    attn = jnp.einsum('bhnik,bhnjk->bhnij', q, k) * decay
    qg = q * jnp.exp(gc)[..., None]
    kg = k * jnp.exp(gc[..., -1:] - gc)[..., None]
    g_last = jnp.exp(gc[..., -1])

    def step(S, inp):
        u_i, w_i, a_i, qg_i, kg_i, gl_i = inp
        v_new = u_i - jnp.einsum('bhck,bhkv->bhcv', w_i, S)
        o = jnp.einsum('bhck,bhkv->bhcv', qg_i, S) + jnp.einsum('bhcs,bhsv->bhcv', a_i, v_new)
        S = S * gl_i[..., None, None] + jnp.einsum('bhck,bhcv->bhkv', kg_i, v_new)
        return S, o

    xs = tuple(jnp.moveaxis(t, 2, 0) for t in (u, w, attn, qg, kg, g_last))
    S, o = lax.scan(step, s0, xs)
    o = jnp.transpose(o, (1, 0, 3, 2, 4)).reshape(B, T, H, V)
    return S, o


def gdn_mixer(p_ctx, p_lat, conv_w, A_log, dt_bias, onorm_g):
    ctx_t = gdn_prepare(p_ctx, conv_w, A_log, dt_bias)
    lat_t = gdn_prepare(p_lat, conv_w, A_log, dt_bias)
    zero = jnp.zeros((p_lat.shape[0], HB, KB, KB), jnp.float32)
    flip = lambda t: jnp.flip(t, axis=1)

    def run(t, s_f, s_b):
        q, k, v, g, beta, z = t
        B, T = q.shape[:2]
        s_f, o_f = gdn_chunked(q, k, v, g[:, :, 0], beta[:, :, 0], s_f)
        s_b, o_b = gdn_chunked(flip(q), flip(k), flip(v), flip(g[:, :, 1]), flip(beta[:, :, 1]), s_b)
        o = o_f + flip(o_b)
        o = o * lax.rsqrt(jnp.mean(o * o, axis=-1, keepdims=True) + NORM_EPS) * onorm_g
        return o.reshape(B, T, DB) * jax.nn.silu(z.astype(jnp.float32)), s_f, s_b

    y_ctx, s_f, s_b = run(ctx_t, zero, zero)
    y_lat, _, _ = run(lat_t, s_f, s_b)
    return y_ctx, y_lat


WKV_CHUNK = 64
WKV_GROUP = 4
WKV_WIDTH = WKV_GROUP * NA


def _wkv_chunk_body(r_ref, v_ref, kk_ref, lw_ref, k_ref, b_ref, y_ref, st_ref):
    C, W, G = WKV_CHUNK, WKV_WIDTH, WKV_GROUP
    f32, bf16 = jnp.float32, jnp.bfloat16
    fwd = pl.program_id(0) == 0
    sgn = jnp.where(fwd, 1, -1)

    @pl.when(pl.program_id(3) == 0)
    def _():
        st_ref[...] = jnp.zeros_like(st_ref)

    r, v, kk = r_ref[0], v_ref[0], kk_ref[0]
    lw, k, b = lw_ref[0, 0], k_ref[0, 0], b_ref[0, 0]

    row = lax.broadcasted_iota(jnp.int32, (C, C), 0)
    col = lax.broadcasted_iota(jnp.int32, (C, C), 1)
    tri = jnp.where((col - row) * sgn <= 0, 1.0, 0.0).astype(bf16)
    hi = lw.astype(bf16)
    rem = lw - hi.astype(f32)
    mid = rem.astype(bf16)
    lo = (rem - mid.astype(f32)).astype(bf16)
    dot = functools.partial(jnp.dot, preferred_element_type=f32)
    cinc = dot(tri, hi) + dot(tri, mid) + dot(tri, lo)
    ctot = jnp.where(fwd, cinc[C - 1:C], cinc[0:1])
    e_out = jnp.exp(-cinc)
    e_last = jnp.exp(ctot - cinc)

    rr = lax.broadcasted_iota(jnp.int32, (G * C, W), 0)
    cc = lax.broadcasted_iota(jnp.int32, (G * C, W), 1)
    same_head = (rr // C) == (cc // NA)

    def bdiag(x):
        return jnp.where(same_head, jnp.concatenate([x] * G, axis=0), 0.0).astype(bf16)

    lhs = jnp.concatenate([bdiag(kk * jnp.exp(cinc - lw)), bdiag(r * jnp.exp(cinc))], axis=0)
    rhs = jnp.concatenate([bdiag(b * e_out), bdiag(k * e_out)], axis=0)
    trans_b = (((1,), (1,)), ((), ()))
    gram = lax.dot_general(lhs, rhs, trans_b, preferred_element_type=f32)
    st = st_ref[...]
    hs = lax.dot_general(lhs, st.astype(bf16), trans_b, preferred_element_type=f32)

    tr, tc = rr % C, cc % C
    order = (tc - tr) * sgn
    strict = order < 0
    incl = order <= 0
    n = G * C
    a_bk = jnp.where(strict, gram[:n, :n], 0.0)
    a_kk = jnp.where(strict, gram[:n, n:], 0.0)
    rb = jnp.where(incl, gram[n:, :n], 0.0)
    rk = jnp.where(incl, gram[n:, n:], 0.0)

    vbd = bdiag(v)
    x = hs[:n] + dot(a_kk.astype(bf16), vbd)
    p = -a_bk
    steps = C.bit_length() - 1
    for i in range(steps):
        pb = p.astype(bf16)
        x = x + dot(pb, x.astype(bf16))
        if i + 1 < steps:
            p = dot(pb, pb)
    ub = x.astype(bf16)

    y = hs[n:] + dot(jnp.concatenate([rk, -rb], axis=1).astype(bf16), jnp.concatenate([vbd, ub], axis=0))
    y_ref[0, 0] = sum(y[h * C:(h + 1) * C] for h in range(G))

    trans_a = (((0,), (0,)), ((), ()))
    upd = lax.dot_general(jnp.concatenate([vbd, -ub], axis=0),
                          jnp.concatenate([bdiag(k * e_last), bdiag(b * e_last)], axis=0),
                          trans_a, preferred_element_type=f32)
    st_ref[...] = st * jnp.exp(ctot) + upd


def wkv7_chunked(r, v, kk, lw, k_dir, b_dir, n_ctx_chunks):
    B, T, da = r.shape
    C, W = WKV_CHUNK, WKV_WIDTH
    n = T // C

    def chunk(d, c):
        back = jnp.where(c < n_ctx_chunks, n_ctx_chunks - 1 - c, n + n_ctx_chunks - 1 - c)
        return jnp.where(d == 0, c, back)

    shared = pl.BlockSpec((1, C, W), lambda d, bi, g, c: (bi, chunk(d, c), g))
    per_dir = pl.BlockSpec((1, 1, C, W), lambda d, bi, g, c: (d, bi, chunk(d, c), g))
    return pl.pallas_call(
        _wkv_chunk_body,
        grid=(2, B, da // W, n),
        in_specs=[shared, shared, shared, per_dir, per_dir, per_dir],
        out_specs=per_dir,
        out_shape=jax.ShapeDtypeStruct((2, B, T, da), jnp.float32),
        scratch_shapes=[pltpu.VMEM((W, W), jnp.float32)],
        compiler_params=pltpu.CompilerParams(dimension_semantics=("arbitrary",) * 4,
                                             vmem_limit_bytes=VMEM_LIMIT),
    )(r, v, kk, lw, k_dir, b_dir)


def rwkv7_prepare_bm(p, shifted, mu, w0, w_up, a0, a_up, g_up, k_k, k_a):
    p = (p + mu * (shifted - p)).astype(jnp.float32)
    B, T = p.shape[:2]
    cuts = [DA, 2 * DA, 3 * DA, 3 * DA + 2 * DECAY_LORA, 3 * DA + 2 * DECAY_LORA + 2 * AAA_LORA]
    r, k, v, wl, al, gl = jnp.split(p, cuts, axis=-1)
    wl = wl.reshape(B, T, 2, DECAY_LORA)
    al = al.reshape(B, T, 2, AAA_LORA)
    w_log = -jax.nn.softplus(-(w0[:, None, None] + jnp.einsum('btdl,dlc->dbtc', jnp.tanh(wl), w_up))) - 0.5
    lw = -jnp.exp(w_log)
    a = jax.nn.sigmoid(a0[:, None, None] + jnp.einsum('btdl,dlc->dbtc', al, a_up))
    g = jax.nn.sigmoid(gl) @ g_up
    kk = l2norm((k * k_k).reshape(B, T, HA, NA)).reshape(B, T, DA)
    k_dir = k[None] * (1.0 + (a - 1.0) * k_a)
    b_dir = kk[None] * a
    return r, v, kk, g, lw, k_dir, b_dir


def rwkv7_mixer_latent(p_ctx, p_lat, mu, w0, w_up, a0, a_up, g_up, k_k, k_a, r_k, lnx_w, lnx_b):
    t_ctx = p_ctx.shape[1]
    p = jnp.concatenate([p_ctx, p_lat], axis=1)
    shifted = jnp.concatenate([shift_seq(p_ctx), shift_grid(p_lat)], axis=1)
    r, v, kk, g, lw, k_dir, b_dir = rwkv7_prepare_bm(p, shifted, mu, w0, w_up, a0, a_up, g_up, k_k, k_a)
    y2 = wkv7_chunked(r, v, kk, lw, k_dir, b_dir, t_ctx // WKV_CHUNK)
    sl = lambda t: t[..., t_ctx:, :]
    r, v, g, k_dir = sl(r), sl(v), sl(g), sl(k_dir)
    B, T = r.shape[:2]
    y = (sl(y2[0]) + sl(y2[1])).reshape(B, T, HA, NA)
    m = jnp.mean(y, axis=-1, keepdims=True)
    var = jnp.mean(jnp.square(y - m), axis=-1, keepdims=True)
    y = ((y - m) * lax.rsqrt(var + RWKV_EPS)).reshape(B, T, DA) * lnx_w + lnx_b
    heads = lambda t: t.reshape(t.shape[:-1] + (HA, NA))
    bonus = jnp.sum(heads(r)[None] * heads(k_dir) * r_k, axis=(0, 4))
    return (y + (bonus[..., None] * heads(v)).reshape(B, T, DA)) * g


ROW_BLOCK = 256
COMBINE_ROWS = 128
VMEM_LIMIT = 48 * 1024 * 1024


def moe_route(xs, router_w, router_b):
    N = xs.shape[0]
    E = router_w.shape[1]
    scores = jax.nn.sigmoid((xs @ router_w).astype(jnp.float32))
    sel = scores + router_b.astype(jnp.float32)
    grp_score = jnp.sum(lax.top_k(sel.reshape(N, N_GROUPS, E // N_GROUPS), 2)[0], axis=-1)
    top_g = lax.top_k(grp_score, TOPK_GROUPS)[1]
    gmask = jnp.any(top_g[:, :, None] == jnp.arange(N_GROUPS)[None, None, :], axis=1)
    emask = jnp.repeat(gmask, E // N_GROUPS, axis=1)
    top_e = lax.top_k(jnp.where(emask, sel, -jnp.inf), TOP_K)[1]
    wts = jnp.take_along_axis(scores, top_e, axis=1)
    wts = wts / jnp.sum(wts, axis=-1, keepdims=True) * ROUTED_SCALE
    return top_e, wts


def moe_layout(top_e, n_experts):
    N, K = top_e.shape
    nk = N * K
    e_flat = top_e.reshape(nk).astype(jnp.int32)
    e_sorted, order = lax.sort((e_flat, jnp.arange(nk, dtype=jnp.int32)), num_keys=1, is_stable=True)
    eids = jnp.arange(n_experts, dtype=jnp.int32)
    cend = jnp.searchsorted(e_sorted, eids, side='right').astype(jnp.int32)
    cstart = jnp.concatenate([jnp.zeros((1,), jnp.int32), cend[:-1]])
    cnt = cend - cstart
    padded = (cnt + ROW_BLOCK - 1) // ROW_BLOCK * ROW_BLOCK
    pend = jnp.cumsum(padded).astype(jnp.int32)
    shift = (pend - padded) - cstart
    onehot = e_sorted[:, None] == eids[None, :]
    dest = jnp.arange(nk, dtype=jnp.int32) + jnp.sum(jnp.where(onehot, shift[None, :], 0), axis=1)
    n_blk = (nk + n_experts * (ROW_BLOCK - 1) + ROW_BLOCK - 1) // ROW_BLOCK
    row_tok = jnp.zeros((n_blk * ROW_BLOCK,), jnp.int32).at[dest].set(order // K)
    pos = jnp.zeros((nk,), jnp.int32).at[order].set(dest).reshape(N, K)
    blk_start = jnp.arange(n_blk, dtype=jnp.int32) * ROW_BLOCK
    blk_e = jnp.minimum(jnp.searchsorted(pend, blk_start, side='right'), n_experts - 1).astype(jnp.int32)
    n_valid_blk = (pend[-1] // ROW_BLOCK).astype(jnp.int32).reshape(1)
    return blk_e, n_valid_blk, row_tok, pos


def _expert_ffn_body(blk_e_ref, nvb_ref, tok_cur_ref, tok_nxt_ref, x_hbm, w13_ref, w2_ref, y_ref, xbuf, sem):
    i = pl.program_id(0)
    nvb = nvb_ref[0]
    slot = lax.rem(i, 2)
    de = w2_ref.shape[1]

    def row_copy(tok_ref, r, s):
        t = tok_ref[0, 0, r]
        return pltpu.make_async_copy(x_hbm.at[pl.ds(t, 1)], xbuf.at[s, pl.ds(r, 1)], sem.at[s])

    def start_gather(tok_ref, s):
        def body(r, c):
            row_copy(tok_ref, r, s).start()
            return c
        lax.fori_loop(0, ROW_BLOCK, body, 0, unroll=8)

    @pl.when((i == 0) & (nvb > 0))
    def _():
        start_gather(tok_cur_ref, 0)

    @pl.when(i + 1 < nvb)
    def _():
        start_gather(tok_nxt_ref, 1 - slot)

    @pl.when(i < nvb)
    def _():
        def body(r, c):
            row_copy(tok_cur_ref, r, slot).wait()
            return c
        lax.fori_loop(0, ROW_BLOCK, body, 0, unroll=8)
        xb = xbuf[slot].astype(jnp.bfloat16)
        h13 = jnp.dot(xb, w13_ref[0], preferred_element_type=jnp.float32)
        h = (jax.nn.silu(h13[:, :de]) * h13[:, de:]).astype(jnp.bfloat16)
        y_ref[...] = jnp.dot(h, w2_ref[0], preferred_element_type=jnp.float32)

    @pl.when(i >= nvb)
    def _():
        y_ref[...] = jnp.zeros_like(y_ref)


def expert_ffn_rows(xs, blk_e, n_valid_blk, row_tok, w13, w2):
    N, D = xs.shape
    n_blk = blk_e.shape[0]
    de2 = w13.shape[2]
    tok3 = row_tok.reshape(n_blk, 1, ROW_BLOCK)
    last = n_blk - 1
    grid_spec = pltpu.PrefetchScalarGridSpec(
        num_scalar_prefetch=2,
        grid=(n_blk,),
        in_specs=[
            pl.BlockSpec((1, 1, ROW_BLOCK), lambda i, be, nv: (i, 0, 0), memory_space=pltpu.SMEM),
            pl.BlockSpec((1, 1, ROW_BLOCK), lambda i, be, nv: (jnp.minimum(i + 1, last), 0, 0),
                         memory_space=pltpu.SMEM),
            pl.BlockSpec(memory_space=pl.ANY),
            pl.BlockSpec((1, D, de2), lambda i, be, nv: (be[i], 0, 0)),
            pl.BlockSpec((1, de2 // 2, D), lambda i, be, nv: (be[i], 0, 0)),
        ],
        out_specs=pl.BlockSpec((ROW_BLOCK, D), lambda i, be, nv: (i, 0)),
        scratch_shapes=[pltpu.VMEM((2, ROW_BLOCK, D), jnp.float32), pltpu.SemaphoreType.DMA((2,))],
    )
    return pl.pallas_call(
        _expert_ffn_body,
        grid_spec=grid_spec,
        out_shape=jax.ShapeDtypeStruct((n_blk * ROW_BLOCK, D), jnp.float32),
        compiler_params=pltpu.CompilerParams(dimension_semantics=("arbitrary",), vmem_limit_bytes=VMEM_LIMIT),
    )(blk_e, n_valid_blk, tok3, tok3, xs, w13, w2)


def _combine_body(pos_cur_ref, pos_nxt_ref, y_hbm, wts_ref, hin_ref, hres_ref, gt_ref, sw13_ref, sw2_ref,
                  fg_ref, out_ref, ybuf, sem):
    j = pl.program_id(0)
    nt = pl.num_programs(0)
    slot = lax.rem(j, 2)
    K = wts_ref.shape[1]
    tm = wts_ref.shape[0]
    ds = sw2_ref.shape[0]

    def row_copy(pos_ref, k, r, s):
        p = pos_ref[0, 0, k * tm + r]
        return pltpu.make_async_copy(y_hbm.at[pl.ds(p, 1)], ybuf.at[s, k, pl.ds(r, 1)], sem.at[s])

    def start_gather(pos_ref, s):
        for k in range(K):
            def body(r, c):
                row_copy(pos_ref, k, r, s).start()
                return c
            lax.fori_loop(0, tm, body, 0, unroll=8)

    @pl.when(j == 0)
    def _():
        start_gather(pos_cur_ref, 0)

    @pl.when(j + 1 < nt)
    def _():
        start_gather(pos_nxt_ref, 1 - slot)

    for k in range(K):
        def body(r, c):
            row_copy(pos_cur_ref, k, r, slot).wait()
            return c
        lax.fori_loop(0, tm, body, 0, unroll=8)

    w = wts_ref[...]
    acc = ybuf[slot, 0] * w[:, 0:1]
    for k in range(1, K):
        acc = acc + ybuf[slot, k] * w[:, k:k + 1]
    hb = hin_ref[...].astype(jnp.bfloat16)
    s13 = jnp.dot(hb, sw13_ref[...], preferred_element_type=jnp.float32)
    sh = (jax.nn.silu(s13[:, :ds]) * s13[:, ds:]).astype(jnp.bfloat16)
    acc = acc + jnp.dot(sh, sw2_ref[...], preferred_element_type=jnp.float32)
    h = hres_ref[...] + gt_ref[0] * acc
    ms = jnp.mean(h * h, axis=-1, keepdims=True)
    out_ref[...] = h * lax.rsqrt(ms + NORM_EPS) * fg_ref[...]


def moe_combine_norm(y_rows, pos, wts, h_in, h_res, gate, sw13, sw2, final_g, seq):
    N, D = h_in.shape
    K = pos.shape[1]
    tm = COMBINE_ROWS
    nt = N // tm
    tiles_per_seq = seq // tm
    pos3 = pos.reshape(nt, tm, K).transpose(0, 2, 1).reshape(nt, 1, K * tm)
    last = nt - 1
    return pl.pallas_call(
        _combine_body,
        grid=(nt,),
        in_specs=[
            pl.BlockSpec((1, 1, K * tm), lambda j: (j, 0, 0), memory_space=pltpu.SMEM),
            pl.BlockSpec((1, 1, K * tm), lambda j: (jnp.minimum(j + 1, last), 0, 0), memory_space=pltpu.SMEM),
            pl.BlockSpec(memory_space=pl.ANY),
            pl.BlockSpec((tm, K), lambda j: (j, 0)),
            pl.BlockSpec((tm, D), lambda j: (j, 0)),
            pl.BlockSpec((tm, D), lambda j: (j, 0)),
            pl.BlockSpec((1, 1, D), lambda j: (j // tiles_per_seq, 0, 0)),
            pl.BlockSpec(sw13.shape, lambda j: (0, 0)),
            pl.BlockSpec(sw2.shape, lambda j: (0, 0)),
            pl.BlockSpec((1, D), lambda j: (0, 0)),
        ],
        out_specs=pl.BlockSpec((tm, D), lambda j: (j, 0)),
        out_shape=jax.ShapeDtypeStruct((N, D), jnp.float32),
        scratch_shapes=[pltpu.VMEM((2, K, tm, D), jnp.float32), pltpu.SemaphoreType.DMA((2,))],
        compiler_params=pltpu.CompilerParams(dimension_semantics=("arbitrary",), vmem_limit_bytes=VMEM_LIMIT),
    )(pos3, pos3, y_rows, wts, h_in, h_res, gate.reshape(-1, 1, D), sw13, sw2, final_g.reshape(1, D))


def moe_block_final(h_in, h_res, gate, router_w, router_b, w1, w3, w2, sw1, sw3, sw2, final_g, seq):
    bf = jnp.bfloat16
    top_e, wts = moe_route(h_in, router_w, router_b)
    blk_e, n_valid_blk, row_tok, pos = moe_layout(top_e, router_w.shape[1])
    w13 = jnp.concatenate([w1, w3], axis=-1).astype(bf)
    y_rows = expert_ffn_rows(h_in, blk_e, n_valid_blk, row_tok, w13, w2.astype(bf))
    sw13 = jnp.concatenate([sw1, sw3], axis=-1).astype(bf)
    return moe_combine_norm(y_rows, pos, wts, h_in, h_res, gate, sw13, sw2.astype(bf), final_g, seq)


def kernel(x, c, ctx, c_ctx, w_ada, b_ada, norm1_g, w_in, mu_shift, w0, w_up, a0, a_up, g_up,
           k_k, k_a, r_k, lnx_w, lnx_b, conv_w, A_log, dt_bias, onorm_g, w_out, norm2_g,
           router_w, router_b, exp_w1, exp_w3, exp_w2, sh_w1, sh_w3, sh_w2, final_g):
    h_x, h_c = x, ctx
    l = 0
    mod = jax.nn.silu(c) @ w_ada[l] + b_ada[l]
    mod_c = jax.nn.silu(c_ctx) @ w_ada[l] + b_ada[l]
    sh1, sc1, gt1, sh2, sc2, gt2 = (m[:, None] for m in jnp.split(mod, 6, axis=-1))
    sh1c, sc1c, gt1c, sh2c, sc2c, gt2c = jnp.split(mod_c, 6, axis=-1)

    p_x = (rmsnorm(h_x, norm1_g[l]) * (1.0 + sc1) + sh1) @ w_in[l]
    p_c = (rmsnorm(h_c, norm1_g[l]) * (1.0 + sc1c) + sh1c) @ w_in[l]
    ya_x = rwkv7_mixer_latent(p_c[..., :RWKV_COLS], p_x[..., :RWKV_COLS], mu_shift[l], w0[l], w_up[l],
                              a0[l], a_up[l], g_up[l], k_k[l], k_a[l], r_k[l], lnx_w[l], lnx_b[l])
    yb_c, yb_x = gdn_mixer(p_c[..., RWKV_COLS:], p_x[..., RWKV_COLS:], conv_w[l], A_log[l],
                           dt_bias[l], onorm_g[l])
    h_x = h_x + gt1 * (jnp.concatenate([ya_x, yb_x], axis=-1).astype(x.dtype) @ w_out[l])
    B, T, D = h_x.shape
    h_in = (rmsnorm(h_x, norm2_g[l]) * (1.0 + sc2) + sh2).reshape(B * T, D)
    out = moe_block_final(h_in, h_x.reshape(B * T, D), gt2[:, 0], router_w[l], router_b[l],
                          exp_w1[l], exp_w3[l], exp_w2[l], sh_w1[l], sh_w3[l], sh_w2[l], final_g, T)
    return out.reshape(B, T, D)
```

```python
import functools

import jax
import jax.numpy as jnp
from jax import lax
from jax.experimental import pallas as pl
from jax.experimental.pallas import tpu as pltpu

D_MODEL = 1024
BATCH = 32
SEQ = 2048
DEPTH = 1
CTX_LEN = 256
GRID_W = 64
D_MIX = D_MODEL
DA = D_MIX // 2
NA = 64
HA = DA // NA
DECAY_LORA = 64
AAA_LORA = 64
GATE_LORA = 128
RWKV_COLS = 3 * DA + 2 * DECAY_LORA + 2 * AAA_LORA + GATE_LORA
RWKV_EPS = 64e-5
DB = D_MIX - DA
HB = 4
KB = DB // HB
GDN_CONV = 5
GDN_CHUNK = 64
GDN_COLS = 4 * DB + 4 * HB
IN_COLS = RWKV_COLS + GDN_COLS
N_EXPERTS = 256
TOP_K = 8
N_GROUPS = 8
TOPK_GROUPS = 4
D_EXPERT = 256
D_SHARED = 256
ROUTED_SCALE = 2.5
MOE_BLOCK = 256
NORM_EPS = 1e-6


def rmsnorm(x, g):
    xf = x.astype(jnp.float32)
    y = xf * lax.rsqrt(jnp.mean(xf * xf, axis=-1, keepdims=True) + NORM_EPS)
    return (y * g.astype(jnp.float32)).astype(x.dtype)


def l2norm(x):
    xf = x.astype(jnp.float32)
    return xf * lax.rsqrt(jnp.sum(xf * xf, axis=-1, keepdims=True) + 1e-6)


def shift_seq(p):
    h = p.shape[-1] // 2
    prev = jnp.pad(p[:, :-1, :h], ((0, 0), (1, 0), (0, 0)))
    nxt = jnp.pad(p[:, 1:, h:], ((0, 0), (0, 1), (0, 0)))
    return jnp.concatenate([prev, nxt], axis=-1)


def shift_grid(p):
    B, T, C = p.shape
    rows = T // GRID_W
    q = C // 4
    g = p.reshape(B, rows, GRID_W, C)
    left = jnp.pad(g[:, :, :-1, :q], ((0, 0), (0, 0), (1, 0), (0, 0)))
    right = jnp.pad(g[:, :, 1:, q:2 * q], ((0, 0), (0, 0), (0, 1), (0, 0)))
    up = jnp.pad(g[:, :-1, :, 2 * q:3 * q], ((0, 0), (1, 0), (0, 0), (0, 0)))
    down = jnp.pad(g[:, 1:, :, 3 * q:], ((0, 0), (0, 1), (0, 0), (0, 0)))
    return jnp.concatenate([left, right, up, down], axis=-1).reshape(B, T, C)


def dwconv_centered(x, w):
    C = x.shape[-1]
    return lax.conv_general_dilated(
        x, w[:, None, :].astype(x.dtype), window_strides=(1,),
        padding=[(GDN_CONV // 2, GDN_CONV // 2)],
        dimension_numbers=('NWC', 'WIO', 'NWC'), feature_group_count=C)


def gdn_prepare(p, conv_w, A_log, dt_bias):
    B, T, _ = p.shape
    qkv = jax.nn.silu(dwconv_centered(p[..., :3 * DB], conv_w)).astype(jnp.float32)
    q, k, v = jnp.split(qkv, 3, axis=-1)
    q = l2norm(q.reshape(B, T, HB, KB)) * (KB ** -0.5)
    k = l2norm(k.reshape(B, T, HB, KB))
    v = v.reshape(B, T, HB, KB)
    z = p[..., 3 * DB:4 * DB]
    gl = p[..., 4 * DB:].astype(jnp.float32).reshape(B, T, 2, 2, HB)
    g = -jnp.exp(A_log) * jax.nn.softplus(gl[:, :, 0] + dt_bias)
    beta = jax.nn.sigmoid(gl[:, :, 1])
    return q, k, v, g, beta, z


WKV_CHUNK = 64
WKV_GROUP = 4
WKV_WIDTH = WKV_GROUP * NA


def _wkv_chunk_body(r_ref, v_ref, kk_ref, lw_ref, k_ref, b_ref, y_ref, st_ref):
    C, W, G = WKV_CHUNK, WKV_WIDTH, WKV_GROUP
    f32, bf16 = jnp.float32, jnp.bfloat16
    fwd = pl.program_id(0) == 0
    sgn = jnp.where(fwd, 1, -1)

    @pl.when(pl.program_id(3) == 0)
    def _():
        st_ref[...] = jnp.zeros_like(st_ref)

    r, v, kk = r_ref[0], v_ref[0], kk_ref[0]
    lw, k, b = lw_ref[0, 0], k_ref[0, 0], b_ref[0, 0]

    row = lax.broadcasted_iota(jnp.int32, (C, C), 0)
    col = lax.broadcasted_iota(jnp.int32, (C, C), 1)
    tri = jnp.where((col - row) * sgn <= 0, 1.0, 0.0).astype(bf16)
    hi = lw.astype(bf16)
    rem = lw - hi.astype(f32)
    mid = rem.astype(bf16)
    lo = (rem - mid.astype(f32)).astype(bf16)
    dot = functools.partial(jnp.dot, preferred_element_type=f32)
    cinc = dot(tri, hi) + dot(tri, mid) + dot(tri, lo)
    ctot = jnp.where(fwd, cinc[C - 1:C], cinc[0:1])
    e_out = jnp.exp(-cinc)
    e_last = jnp.exp(ctot - cinc)

    rr = lax.broadcasted_iota(jnp.int32, (G * C, W), 0)
    cc = lax.broadcasted_iota(jnp.int32, (G * C, W), 1)
    same_head = (rr // C) == (cc // NA)

    def bdiag(x):
        return jnp.where(same_head, jnp.concatenate([x] * G, axis=0), 0.0).astype(bf16)

    lhs = jnp.concatenate([bdiag(kk * jnp.exp(cinc - lw)), bdiag(r * jnp.exp(cinc))], axis=0)
    rhs = jnp.concatenate([bdiag(b * e_out), bdiag(k * e_out)], axis=0)
    trans_b = (((1,), (1,)), ((), ()))
    gram = lax.dot_general(lhs, rhs, trans_b, preferred_element_type=f32)
    st = st_ref[...]
    hs = lax.dot_general(lhs, st.astype(bf16), trans_b, preferred_element_type=f32)

    tr, tc = rr % C, cc % C
    order = (tc - tr) * sgn
    strict = order < 0
    incl = order <= 0
    n = G * C
    a_bk = jnp.where(strict, gram[:n, :n], 0.0)
    a_kk = jnp.where(strict, gram[:n, n:], 0.0)
    rb = jnp.where(incl, gram[n:, :n], 0.0)
    rk = jnp.where(incl, gram[n:, n:], 0.0)

    vbd = bdiag(v)
    x = hs[:n] + dot(a_kk.astype(bf16), vbd)
    p = -a_bk
    steps = C.bit_length() - 1
    for i in range(steps):
        pb = p.astype(bf16)
        x = x + dot(pb, x.astype(bf16))
        if i + 1 < steps:
            p = dot(pb, pb)
    ub = x.astype(bf16)

    y = hs[n:] + dot(jnp.concatenate([rk, -rb], axis=1).astype(bf16), jnp.concatenate([vbd, ub], axis=0))
    y_ref[0, 0] = sum(y[h * C:(h + 1) * C] for h in range(G))

    trans_a = (((0,), (0,)), ((), ()))
    upd = lax.dot_general(jnp.concatenate([vbd, -ub], axis=0),
                          jnp.concatenate([bdiag(k * e_last), bdiag(b * e_last)], axis=0),
                          trans_a, preferred_element_type=f32)
    st_ref[...] = st * jnp.exp(ctot) + upd


def wkv7_chunked(r, v, kk, lw, k_dir, b_dir, n_ctx_chunks):
    B, T, da = r.shape
    C, W = WKV_CHUNK, WKV_WIDTH
    n = T // C

    def chunk(d, c):
        back = jnp.where(c < n_ctx_chunks, n_ctx_chunks - 1 - c, n + n_ctx_chunks - 1 - c)
        return jnp.where(d == 0, c, back)

    shared = pl.BlockSpec((1, C, W), lambda d, bi, g, c: (bi, chunk(d, c), g))
    per_dir = pl.BlockSpec((1, 1, C, W), lambda d, bi, g, c: (d, bi, chunk(d, c), g))
    return pl.pallas_call(
        _wkv_chunk_body,
        grid=(2, B, da // W, n),
        in_specs=[shared, shared, shared, per_dir, per_dir, per_dir],
        out_specs=per_dir,
        out_shape=jax.ShapeDtypeStruct((2, B, T, da), jnp.float32),
        scratch_shapes=[pltpu.VMEM((W, W), jnp.float32)],
        compiler_params=pltpu.CompilerParams(dimension_semantics=("arbitrary",) * 4,
                                             vmem_limit_bytes=VMEM_LIMIT),
    )(r, v, kk, lw, k_dir, b_dir)


def rwkv7_prepare_bm(p, shifted, mu, w0, w_up, a0, a_up, g_up, k_k, k_a):
    p = (p + mu * (shifted - p)).astype(jnp.float32)
    B, T = p.shape[:2]
    cuts = [DA, 2 * DA, 3 * DA, 3 * DA + 2 * DECAY_LORA, 3 * DA + 2 * DECAY_LORA + 2 * AAA_LORA]
    r, k, v, wl, al, gl = jnp.split(p, cuts, axis=-1)
    wl = wl.reshape(B, T, 2, DECAY_LORA)
    al = al.reshape(B, T, 2, AAA_LORA)
    w_log = -jax.nn.softplus(-(w0[:, None, None] + jnp.einsum('btdl,dlc->dbtc', jnp.tanh(wl), w_up))) - 0.5
    lw = -jnp.exp(w_log)
    a = jax.nn.sigmoid(a0[:, None, None] + jnp.einsum('btdl,dlc->dbtc', al, a_up))
    g = jax.nn.sigmoid(gl) @ g_up
    kk = l2norm((k * k_k).reshape(B, T, HA, NA)).reshape(B, T, DA)
    k_dir = k[None] * (1.0 + (a - 1.0) * k_a)
    b_dir = kk[None] * a
    return r, v, kk, g, lw, k_dir, b_dir


def rwkv7_mixer_latent(p_ctx, p_lat, mu, w0, w_up, a0, a_up, g_up, k_k, k_a, r_k, lnx_w, lnx_b):
    t_ctx = p_ctx.shape[1]
    p = jnp.concatenate([p_ctx, p_lat], axis=1)
    shifted = jnp.concatenate([shift_seq(p_ctx), shift_grid(p_lat)], axis=1)
    r, v, kk, g, lw, k_dir, b_dir = rwkv7_prepare_bm(p, shifted, mu, w0, w_up, a0, a_up, g_up, k_k, k_a)
    y2 = wkv7_chunked(r, v, kk, lw, k_dir, b_dir, t_ctx // WKV_CHUNK)
    sl = lambda t: t[..., t_ctx:, :]
    r, v, g, k_dir = sl(r), sl(v), sl(g), sl(k_dir)
    B, T = r.shape[:2]
    y = (sl(y2[0]) + sl(y2[1])).reshape(B, T, HA, NA)
    m = jnp.mean(y, axis=-1, keepdims=True)
    var = jnp.mean(jnp.square(y - m), axis=-1, keepdims=True)
    y = ((y - m) * lax.rsqrt(var + RWKV_EPS)).reshape(B, T, DA) * lnx_w + lnx_b
    heads = lambda t: t.reshape(t.shape[:-1] + (HA, NA))
    bonus = jnp.sum(heads(r)[None] * heads(k_dir) * r_k, axis=(0, 4))
    return (y + (bonus[..., None] * heads(v)).reshape(B, T, DA)) * g


def _gdn_chunk_body(q_ref, k_ref, v_ref, gcx_ref, grow_ref, bx_ref, o_ref, st_ref):
    C, H, KD = GDN_CHUNK, HB, KB
    W, n = H * KD, H * C
    f32, bf16 = jnp.float32, jnp.bfloat16
    fwd = pl.program_id(0) == 0
    sgn = jnp.where(fwd, 1, -1)

    @pl.when(pl.program_id(2) == 0)
    def _():
        st_ref[...] = jnp.zeros_like(st_ref)

    q, k, v = q_ref[0], k_ref[0], v_ref[0]
    gcx, bx = gcx_ref[0, 0], bx_ref[0, 0]
    grow = grow_ref[0, 0, 0]
    dot = functools.partial(jnp.dot, preferred_element_type=f32)
    head = lambda x, h: x[:, h * KD:(h + 1) * KD]

    rr = lax.broadcasted_iota(jnp.int32, (n, W), 0)
    cc = lax.broadcasted_iota(jnp.int32, (n, W), 1)
    same_head = (rr // C) == (cc // KD)

    def bdiag(x):
        return jnp.where(same_head, jnp.concatenate([x] * H, axis=0), 0.0).astype(bf16)

    eg = jnp.exp(gcx)
    kb = k * bx
    trans_b = (((1,), (1,)), ((), ()))
    gram = lax.dot_general(jnp.concatenate([bdiag(kb), bdiag(q)], axis=0), bdiag(k), trans_b,
                           preferred_element_type=f32)

    r2 = lax.broadcasted_iota(jnp.int32, (n, n), 0)
    c2 = lax.broadcasted_iota(jnp.int32, (n, n), 1)
    order = (c2 % C - r2 % C) * sgn
    same2 = (r2 // C) == (c2 // C)
    incl = same2 & (order <= 0)
    strict = same2 & (order < 0)
    g_rows = jnp.concatenate([jnp.concatenate([head(gcx, h)] * (n // KD), axis=1) for h in range(H)], axis=0)
    decay = jnp.exp(jnp.where(incl, g_rows - grow, 0.0))
    a_low = jnp.where(strict, gram[:n] * decay, 0.0)
    attn = jnp.where(incl, gram[n:] * decay, 0.0)

    vb, kbe = v * bx, kb * eg
    x = jnp.concatenate([jnp.concatenate([head(vb, h), head(kbe, h)], axis=1) for h in range(H)], axis=0)
    p = -a_low
    steps = C.bit_length() - 1
    for i in range(steps):
        pb = p.astype(bf16)
        x = x + dot(pb, x.astype(bf16))
        if i + 1 < steps:
            p = dot(pb, pb)
    u, w = x[:, :KD], x[:, KD:]

    st = st_ref[...]
    stb = st.astype(bf16)
    w_bd = jnp.where(same_head, jnp.concatenate([w] * H, axis=1), 0.0).astype(bf16)
    v_new = (u - dot(w_bd, stb)).astype(bf16)
    o = dot(jnp.concatenate([bdiag(q * eg), attn.astype(bf16)], axis=1), jnp.concatenate([stb, v_new], axis=0))
    o_ref[0, 0] = jnp.concatenate([o[h * C:(h + 1) * C] for h in range(H)], axis=1)

    g_last = jnp.where(fwd, gcx[C - 1:C], gcx[0:1])
    trans_a = (((0,), (0,)), ((), ()))
    upd = lax.dot_general(bdiag(k * jnp.exp(g_last - gcx)), v_new, trans_a, preferred_element_type=f32)
    e_last = jnp.exp(g_last)
    st_ref[...] = jnp.concatenate([st[h * KD:(h + 1) * KD] * head(e_last, h) for h in range(H)], axis=0) + upd


def gdn_chunked_pallas(q, k, v, gcx, grow, bx, n_ctx_chunks):
    B, T, W = q.shape
    C = GDN_CHUNK
    n = T // C

    def chunk(d, c):
        back = jnp.where(c < n_ctx_chunks, n_ctx_chunks - 1 - c, n + n_ctx_chunks - 1 - c)
        return jnp.where(d == 0, c, back)

    shared = pl.BlockSpec((1, C, W), lambda d, bi, c: (bi, chunk(d, c), 0))
    per_dir = pl.BlockSpec((1, 1, C, W), lambda d, bi, c: (d, bi, chunk(d, c), 0))
    rows = pl.BlockSpec((1, 1, 1, 1, HB * C), lambda d, bi, c: (d, bi, chunk(d, c), 0, 0))
    return pl.pallas_call(
        _gdn_chunk_body,
        grid=(2, B, n),
        in_specs=[shared, shared, shared, per_dir, rows, per_dir],
        out_specs=per_dir,
        out_shape=jax.ShapeDtypeStruct((2, B, T, W), jnp.float32),
        scratch_shapes=[pltpu.VMEM((W, KB), jnp.float32)],
        compiler_params=pltpu.CompilerParams(dimension_semantics=("arbitrary",) * 3,
                                             vmem_limit_bytes=VMEM_LIMIT),
    )(q, k, v, gcx, grow, bx)


def gdn_mixer_latent(p_ctx, p_lat, conv_w, A_log, dt_bias, onorm_g):
    t_ctx = p_ctx.shape[1]
    qc, kc, vc, gc_, bc, _ = gdn_prepare(p_ctx, conv_w, A_log, dt_bias)
    ql, kl, vl, gl_, bl, z = gdn_prepare(p_lat, conv_w, A_log, dt_bias)
    cat = lambda a, b: jnp.concatenate([a, b], axis=1)
    B, T = ql.shape[0], t_ctx + ql.shape[1]
    C = GDN_CHUNK
    flat = lambda t: t.reshape(B, T, DB)
    q, k, v = flat(cat(qc, ql)), flat(cat(kc, kl)), flat(cat(vc, vl))
    g = jnp.moveaxis(cat(gc_, gl_), 2, 0).reshape(2, B, T // C, C, HB)
    gcs = jnp.stack([jnp.cumsum(g[0], axis=2), lax.cumsum(g[1], axis=2, reverse=True)])
    gcx = jnp.repeat(gcs.reshape(2, B, T, HB), KB, axis=-1)
    grow = jnp.swapaxes(gcs, 3, 4).reshape(2, B, T // C, 1, HB * C)
    bx = jnp.repeat(jnp.moveaxis(cat(bc, bl), 2, 0), KB, axis=-1)
    o2 = gdn_chunked_pallas(q, k, v, gcx, grow, bx, t_ctx // C)
    o = (o2[0, :, t_ctx:] + o2[1, :, t_ctx:]).reshape(B, T - t_ctx, HB, KB)
    o = o * lax.rsqrt(jnp.mean(o * o, axis=-1, keepdims=True) + NORM_EPS) * onorm_g
    return o.reshape(B, T - t_ctx, DB) * jax.nn.silu(z.astype(jnp.float32))


ROW_BLOCK = 256
COMBINE_ROWS = 128
VMEM_LIMIT = 48 * 1024 * 1024


def moe_route(xs, router_w, router_b):
    N = xs.shape[0]
    E = router_w.shape[1]
    scores = jax.nn.sigmoid((xs @ router_w).astype(jnp.float32))
    sel = scores + router_b.astype(jnp.float32)
    grp_score = jnp.sum(lax.top_k(sel.reshape(N, N_GROUPS, E // N_GROUPS), 2)[0], axis=-1)
    top_g = lax.top_k(grp_score, TOPK_GROUPS)[1]
    gmask = jnp.any(top_g[:, :, None] == jnp.arange(N_GROUPS)[None, None, :], axis=1)
    emask = jnp.repeat(gmask, E // N_GROUPS, axis=1)
    top_e = lax.top_k(jnp.where(emask, sel, -jnp.inf), TOP_K)[1]
    wts = jnp.take_along_axis(scores, top_e, axis=1)
    wts = wts / jnp.sum(wts, axis=-1, keepdims=True) * ROUTED_SCALE
    return top_e, wts


def moe_layout(top_e, n_experts):
    N, K = top_e.shape
    nk = N * K
    e_flat = top_e.reshape(nk).astype(jnp.int32)
    e_sorted, order = lax.sort((e_flat, jnp.arange(nk, dtype=jnp.int32)), num_keys=1, is_stable=True)
    eids = jnp.arange(n_experts, dtype=jnp.int32)
    cend = jnp.searchsorted(e_sorted, eids, side='right').astype(jnp.int32)
    cstart = jnp.concatenate([jnp.zeros((1,), jnp.int32), cend[:-1]])
    cnt = cend - cstart
    padded = (cnt + ROW_BLOCK - 1) // ROW_BLOCK * ROW_BLOCK
    pend = jnp.cumsum(padded).astype(jnp.int32)
    shift = (pend - padded) - cstart
    onehot = e_sorted[:, None] == eids[None, :]
    dest = jnp.arange(nk, dtype=jnp.int32) + jnp.sum(jnp.where(onehot, shift[None, :], 0), axis=1)
    n_blk = (nk + n_experts * (ROW_BLOCK - 1) + ROW_BLOCK - 1) // ROW_BLOCK
    row_tok = jnp.zeros((n_blk * ROW_BLOCK,), jnp.int32).at[dest].set(order // K)
    pos = jnp.zeros((nk,), jnp.int32).at[order].set(dest).reshape(N, K)
    blk_start = jnp.arange(n_blk, dtype=jnp.int32) * ROW_BLOCK
    blk_e = jnp.minimum(jnp.searchsorted(pend, blk_start, side='right'), n_experts - 1).astype(jnp.int32)
    n_valid_blk = (pend[-1] // ROW_BLOCK).astype(jnp.int32).reshape(1)
    return blk_e, n_valid_blk, row_tok, pos


def _expert_ffn_body(blk_e_ref, nvb_ref, tok_cur_ref, tok_nxt_ref, x_hbm, w13_ref, w2_ref, y_ref, xbuf, sem):
    i = pl.program_id(0)
    nvb = nvb_ref[0]
    slot = lax.rem(i, 2)
    de = w2_ref.shape[1]

    def row_copy(tok_ref, r, s):
        t = tok_ref[0, 0, r]
        return pltpu.make_async_copy(x_hbm.at[pl.ds(t, 1)], xbuf.at[s, pl.ds(r, 1)], sem.at[s])

    def start_gather(tok_ref, s):
        def body(r, c):
            row_copy(tok_ref, r, s).start()
            return c
        lax.fori_loop(0, ROW_BLOCK, body, 0, unroll=8)

    @pl.when((i == 0) & (nvb > 0))
    def _():
        start_gather(tok_cur_ref, 0)

    @pl.when(i + 1 < nvb)
    def _():
        start_gather(tok_nxt_ref, 1 - slot)

    @pl.when(i < nvb)
    def _():
        pltpu.make_async_copy(x_hbm.at[pl.ds(0, ROW_BLOCK)], xbuf.at[slot], sem.at[slot]).wait()
        xb = xbuf[slot].astype(jnp.bfloat16)
        h13 = jnp.dot(xb, w13_ref[0], preferred_element_type=jnp.float32)
        h = (jax.nn.silu(h13[:, :de]) * h13[:, de:]).astype(jnp.bfloat16)
        y_ref[...] = jnp.dot(h, w2_ref[0], preferred_element_type=jnp.float32)

    @pl.when(i >= nvb)
    def _():
        y_ref[...] = jnp.zeros_like(y_ref)


def expert_ffn_rows(xs, blk_e, n_valid_blk, row_tok, w13, w2):
    N, D = xs.shape
    n_blk = blk_e.shape[0]
    de2 = w13.shape[2]
    tok3 = row_tok.reshape(n_blk, 1, ROW_BLOCK)
    last = n_blk - 1
    grid_spec = pltpu.PrefetchScalarGridSpec(
        num_scalar_prefetch=2,
        grid=(n_blk,),
        in_specs=[
            pl.BlockSpec((1, 1, ROW_BLOCK), lambda i, be, nv: (i, 0, 0), memory_space=pltpu.SMEM),
            pl.BlockSpec((1, 1, ROW_BLOCK), lambda i, be, nv: (jnp.minimum(i + 1, last), 0, 0),
                         memory_space=pltpu.SMEM),
            pl.BlockSpec(memory_space=pl.ANY),
            pl.BlockSpec((1, D, de2), lambda i, be, nv: (be[i], 0, 0)),
            pl.BlockSpec((1, de2 // 2, D), lambda i, be, nv: (be[i], 0, 0)),
        ],
        out_specs=pl.BlockSpec((ROW_BLOCK, D), lambda i, be, nv: (i, 0)),
        scratch_shapes=[pltpu.VMEM((2, ROW_BLOCK, D), jnp.float32), pltpu.SemaphoreType.DMA((2,))],
    )
    return pl.pallas_call(
        _expert_ffn_body,
        grid_spec=grid_spec,
        out_shape=jax.ShapeDtypeStruct((n_blk * ROW_BLOCK, D), jnp.float32),
        compiler_params=pltpu.CompilerParams(dimension_semantics=("arbitrary",), vmem_limit_bytes=VMEM_LIMIT),
    )(blk_e, n_valid_blk, tok3, tok3, xs, w13, w2)


def _combine_body(pos_cur_ref, pos_nxt_ref, y_hbm, wts_ref, hin_ref, hres_ref, gt_ref, sw13_ref, sw2_ref,
                  fg_ref, out_ref, ybuf, sem):
    j = pl.program_id(0)
    nt = pl.num_programs(0)
    slot = lax.rem(j, 2)
    K = wts_ref.shape[1]
    tm = wts_ref.shape[0]
    ds = sw2_ref.shape[0]

    def row_copy(pos_ref, k, r, s):
        p = pos_ref[0, 0, k * tm + r]
        return pltpu.make_async_copy(y_hbm.at[pl.ds(p, 1)], ybuf.at[s, k, pl.ds(r, 1)], sem.at[s])

    def start_gather(pos_ref, s):
        for k in range(K):
            def body(r, c):
                row_copy(pos_ref, k, r, s).start()
                return c
            lax.fori_loop(0, tm, body, 0, unroll=8)

    @pl.when(j == 0)
    def _():
        start_gather(pos_cur_ref, 0)

    @pl.when(j + 1 < nt)
    def _():
        start_gather(pos_nxt_ref, 1 - slot)

    for k in range(K):
        pltpu.make_async_copy(y_hbm.at[pl.ds(0, tm)], ybuf.at[slot, k], sem.at[slot]).wait()

    w = wts_ref[...]
    acc = ybuf[slot, 0] * w[:, 0:1]
    for k in range(1, K):
        acc = acc + ybuf[slot, k] * w[:, k:k + 1]
    hb = hin_ref[...].astype(jnp.bfloat16)
    s13 = jnp.dot(hb, sw13_ref[...], preferred_element_type=jnp.float32)
    sh = (jax.nn.silu(s13[:, :ds]) * s13[:, ds:]).astype(jnp.bfloat16)
    acc = acc + jnp.dot(sh, sw2_ref[...], preferred_element_type=jnp.float32)
    h = hres_ref[...] + gt_ref[0] * acc
    ms = jnp.mean(h * h, axis=-1, keepdims=True)
    out_ref[...] = h * lax.rsqrt(ms + NORM_EPS) * fg_ref[...]


def moe_combine_norm(y_rows, pos, wts, h_in, h_res, gate, sw13, sw2, final_g, seq):
    N, D = h_in.shape
    K = pos.shape[1]
    tm = COMBINE_ROWS
    nt = N // tm
    tiles_per_seq = seq // tm
    pos3 = pos.reshape(nt, tm, K).transpose(0, 2, 1).reshape(nt, 1, K * tm)
    last = nt - 1
    return pl.pallas_call(
        _combine_body,
        grid=(nt,),
        in_specs=[
            pl.BlockSpec((1, 1, K * tm), lambda j: (j, 0, 0), memory_space=pltpu.SMEM),
            pl.BlockSpec((1, 1, K * tm), lambda j: (jnp.minimum(j + 1, last), 0, 0), memory_space=pltpu.SMEM),
            pl.BlockSpec(memory_space=pl.ANY),
            pl.BlockSpec((tm, K), lambda j: (j, 0)),
            pl.BlockSpec((tm, D), lambda j: (j, 0)),
            pl.BlockSpec((tm, D), lambda j: (j, 0)),
            pl.BlockSpec((1, 1, D), lambda j: (j // tiles_per_seq, 0, 0)),
            pl.BlockSpec(sw13.shape, lambda j: (0, 0)),
            pl.BlockSpec(sw2.shape, lambda j: (0, 0)),
            pl.BlockSpec((1, D), lambda j: (0, 0)),
        ],
        out_specs=pl.BlockSpec((tm, D), lambda j: (j, 0)),
        out_shape=jax.ShapeDtypeStruct((N, D), jnp.float32),
        scratch_shapes=[pltpu.VMEM((2, K, tm, D), jnp.float32), pltpu.SemaphoreType.DMA((2,))],
        compiler_params=pltpu.CompilerParams(dimension_semantics=("arbitrary",), vmem_limit_bytes=VMEM_LIMIT),
    )(pos3, pos3, y_rows, wts, h_in, h_res, gate.reshape(-1, 1, D), sw13, sw2, final_g.reshape(1, D))


def moe_block_final(h_in, h_res, gate, router_w, router_b, w1, w3, w2, sw1, sw3, sw2, final_g, seq):
    bf = jnp.bfloat16
    top_e, wts = moe_route(h_in, router_w, router_b)
    blk_e, n_valid_blk, row_tok, pos = moe_layout(top_e, router_w.shape[1])
    w13 = jnp.concatenate([w1, w3], axis=-1).astype(bf)
    y_rows = expert_ffn_rows(h_in, blk_e, n_valid_blk, row_tok, w13, w2.astype(bf))
    sw13 = jnp.concatenate([sw1, sw3], axis=-1).astype(bf)
    return moe_combine_norm(y_rows, pos, wts, h_in, h_res, gate, sw13, sw2.astype(bf), final_g, seq)


def kernel(x, c, ctx, c_ctx, w_ada, b_ada, norm1_g, w_in, mu_shift, w0, w_up, a0, a_up, g_up,
           k_k, k_a, r_k, lnx_w, lnx_b, conv_w, A_log, dt_bias, onorm_g, w_out, norm2_g,
           router_w, router_b, exp_w1, exp_w3, exp_w2, sh_w1, sh_w3, sh_w2, final_g):
    h_x, h_c = x, ctx
    l = 0
    mod = jax.nn.silu(c) @ w_ada[l] + b_ada[l]
    mod_c = jax.nn.silu(c_ctx) @ w_ada[l] + b_ada[l]
    sh1, sc1, gt1, sh2, sc2, gt2 = (m[:, None] for m in jnp.split(mod, 6, axis=-1))
    sh1c, sc1c, gt1c, sh2c, sc2c, gt2c = jnp.split(mod_c, 6, axis=-1)

    p_x = (rmsnorm(h_x, norm1_g[l]) * (1.0 + sc1) + sh1) @ w_in[l]
    p_c = (rmsnorm(h_c, norm1_g[l]) * (1.0 + sc1c) + sh1c) @ w_in[l]
    ya_x = rwkv7_mixer_latent(p_c[..., :RWKV_COLS], p_x[..., :RWKV_COLS], mu_shift[l], w0[l], w_up[l],
                              a0[l], a_up[l], g_up[l], k_k[l], k_a[l], r_k[l], lnx_w[l], lnx_b[l])
    yb_x = gdn_mixer_latent(p_c[..., RWKV_COLS:], p_x[..., RWKV_COLS:], conv_w[l], A_log[l],
                            dt_bias[l], onorm_g[l])
    h_x = h_x + gt1 * (jnp.concatenate([ya_x, yb_x], axis=-1).astype(x.dtype) @ w_out[l])
    B, T, D = h_x.shape
    h_in = (rmsnorm(h_x, norm2_g[l]) * (1.0 + sc2) + sh2).reshape(B * T, D)
    out = moe_block_final(h_in, h_x.reshape(B * T, D), gt2[:, 0], router_w[l], router_b[l],
                          exp_w1[l], exp_w3[l], exp_w2[l], sh_w1[l], sh_w3[l], sh_w2[l], final_g, T)
    return out.reshape(B, T, D)
```

```python
import functools

import jax
import jax.numpy as jnp
from jax import lax
from jax.experimental import pallas as pl
from jax.experimental.pallas import tpu as pltpu

D_MODEL = 1024
BATCH = 32
SEQ = 2048
DEPTH = 1
CTX_LEN = 256
GRID_W = 64
D_MIX = D_MODEL
DA = D_MIX // 2
NA = 64
HA = DA // NA
DECAY_LORA = 64
AAA_LORA = 64
GATE_LORA = 128
RWKV_COLS = 3 * DA + 2 * DECAY_LORA + 2 * AAA_LORA + GATE_LORA
RWKV_EPS = 64e-5
DB = D_MIX - DA
HB = 4
KB = DB // HB
GDN_CONV = 5
GDN_CHUNK = 64
GDN_COLS = 4 * DB + 4 * HB
IN_COLS = RWKV_COLS + GDN_COLS
N_EXPERTS = 256
TOP_K = 8
N_GROUPS = 8
TOPK_GROUPS = 4
D_EXPERT = 256
D_SHARED = 256
ROUTED_SCALE = 2.5
MOE_BLOCK = 256
NORM_EPS = 1e-6


def rmsnorm(x, g):
    xf = x.astype(jnp.float32)
    y = xf * lax.rsqrt(jnp.mean(xf * xf, axis=-1, keepdims=True) + NORM_EPS)
    return (y * g.astype(jnp.float32)).astype(x.dtype)


def l2norm(x):
    xf = x.astype(jnp.float32)
    return xf * lax.rsqrt(jnp.sum(xf * xf, axis=-1, keepdims=True) + 1e-6)


def shift_seq(p):
    h = p.shape[-1] // 2
    prev = jnp.pad(p[:, :-1, :h], ((0, 0), (1, 0), (0, 0)))
    nxt = jnp.pad(p[:, 1:, h:], ((0, 0), (0, 1), (0, 0)))
    return jnp.concatenate([prev, nxt], axis=-1)


def shift_grid(p):
    B, T, C = p.shape
    rows = T // GRID_W
    q = C // 4
    g = p.reshape(B, rows, GRID_W, C)
    left = jnp.pad(g[:, :, :-1, :q], ((0, 0), (0, 0), (1, 0), (0, 0)))
    right = jnp.pad(g[:, :, 1:, q:2 * q], ((0, 0), (0, 0), (0, 1), (0, 0)))
    up = jnp.pad(g[:, :-1, :, 2 * q:3 * q], ((0, 0), (1, 0), (0, 0), (0, 0)))
    down = jnp.pad(g[:, 1:, :, 3 * q:], ((0, 0), (0, 1), (0, 0), (0, 0)))
    return jnp.concatenate([left, right, up, down], axis=-1).reshape(B, T, C)


def dwconv_centered(x, w):
    C = x.shape[-1]
    return lax.conv_general_dilated(
        x, w[:, None, :].astype(x.dtype), window_strides=(1,),
        padding=[(GDN_CONV // 2, GDN_CONV // 2)],
        dimension_numbers=('NWC', 'WIO', 'NWC'), feature_group_count=C)


def gdn_prepare(p, conv_w, A_log, dt_bias):
    B, T, _ = p.shape
    qkv = jax.nn.silu(dwconv_centered(p[..., :3 * DB], conv_w)).astype(jnp.float32)
    q, k, v = jnp.split(qkv, 3, axis=-1)
    q = l2norm(q.reshape(B, T, HB, KB)) * (KB ** -0.5)
    k = l2norm(k.reshape(B, T, HB, KB))
    v = v.reshape(B, T, HB, KB)
    z = p[..., 3 * DB:4 * DB]
    gl = p[..., 4 * DB:].astype(jnp.float32).reshape(B, T, 2, 2, HB)
    g = -jnp.exp(A_log) * jax.nn.softplus(gl[:, :, 0] + dt_bias)
    beta = jax.nn.sigmoid(gl[:, :, 1])
    return q, k, v, g, beta, z


WKV_CHUNK = 64
WKV_GROUP = 4
WKV_WIDTH = WKV_GROUP * NA
WKV_ROWS = 2


def _wkv_chunk_body(r_ref, v_ref, kk_ref, lw_ref, k_ref, b_ref, y_ref, st_ref):
    W = WKV_WIDTH
    fwd = pl.program_id(0) == 0

    @pl.when(pl.program_id(2) == 0)
    def _():
        st_ref[...] = jnp.zeros_like(st_ref)

    rows, n_groups = r_ref.shape[0], r_ref.shape[2] // W
    where = [(i, slice(g * W, (g + 1) * W)) for i in range(rows) for g in range(n_groups)]
    chains = [_wkv_chunk_math(fwd, r_ref[i, :, lanes], v_ref[i, :, lanes], kk_ref[i, :, lanes],
                              lw_ref[0, i, :, lanes], k_ref[0, i, :, lanes], b_ref[0, i, :, lanes], st_ref[s])
              for s, (i, lanes) in enumerate(where)]
    for s, (y, st) in enumerate(_interleave(chains)):
        i, lanes = where[s]
        y_ref[0, i, :, lanes] = y
        st_ref[s] = st


def _interleave(chains):
    results = [None] * len(chains)
    live = list(range(len(chains)))
    while live:
        for i in list(live):
            try:
                next(chains[i])
            except StopIteration as stop:
                results[i] = stop.value
                live.remove(i)
    return results


def _wkv_chunk_math(fwd, r, v, kk, lw, k, b, st):
    C, W, G = WKV_CHUNK, WKV_WIDTH, WKV_GROUP
    f32, bf16 = jnp.float32, jnp.bfloat16
    sgn = jnp.where(fwd, 1, -1)

    row = lax.broadcasted_iota(jnp.int32, (C, C), 0)
    col = lax.broadcasted_iota(jnp.int32, (C, C), 1)
    tri = jnp.where((col - row) * sgn <= 0, 1.0, 0.0).astype(bf16)
    hi = lw.astype(bf16)
    rem = lw - hi.astype(f32)
    mid = rem.astype(bf16)
    lo = (rem - mid.astype(f32)).astype(bf16)
    dot = functools.partial(jnp.dot, preferred_element_type=f32)
    cinc = dot(tri, hi) + dot(tri, mid) + dot(tri, lo)
    yield
    ctot = jnp.where(fwd, cinc[C - 1:C], cinc[0:1])
    e_out = jnp.exp(-cinc)
    e_last = jnp.exp(ctot - cinc)

    rr = lax.broadcasted_iota(jnp.int32, (G * C, W), 0)
    cc = lax.broadcasted_iota(jnp.int32, (G * C, W), 1)
    same_head = (rr // C) == (cc // NA)

    def bdiag(x):
        return jnp.where(same_head, jnp.concatenate([x] * G, axis=0), 0.0).astype(bf16)

    lhs = jnp.concatenate([bdiag(kk * jnp.exp(cinc - lw)), bdiag(r * jnp.exp(cinc))], axis=0)
    rhs = jnp.concatenate([bdiag(b * e_out), bdiag(k * e_out)], axis=0)
    trans_b = (((1,), (1,)), ((), ()))
    gram = lax.dot_general(lhs, rhs, trans_b, preferred_element_type=f32)
    hs = lax.dot_general(lhs, st.astype(bf16), trans_b, preferred_element_type=f32)
    yield

    tr, tc = rr % C, cc % C
    order = (tc - tr) * sgn
    strict = order < 0
    incl = order <= 0
    n = G * C
    a_bk = jnp.where(strict, gram[:n, :n], 0.0)
    a_kk = jnp.where(strict, gram[:n, n:], 0.0)
    rb = jnp.where(incl, gram[n:, :n], 0.0)
    rk = jnp.where(incl, gram[n:, n:], 0.0)

    vbd = bdiag(v)
    x = hs[:n] + dot(a_kk.astype(bf16), vbd)
    p = -a_bk
    steps = C.bit_length() - 1
    for i in range(steps):
        yield
        pb = p.astype(bf16)
        x = x + dot(pb, x.astype(bf16))
        if i + 1 < steps:
            p = dot(pb, pb)
    ub = x.astype(bf16)
    yield

    y = hs[n:] + dot(jnp.concatenate([rk, -rb], axis=1).astype(bf16), jnp.concatenate([vbd, ub], axis=0))
    yield

    trans_a = (((0,), (0,)), ((), ()))
    upd = lax.dot_general(jnp.concatenate([vbd, -ub], axis=0),
                          jnp.concatenate([bdiag(k * e_last), bdiag(b * e_last)], axis=0),
                          trans_a, preferred_element_type=f32)
    return sum(y[h * C:(h + 1) * C] for h in range(G)), st * jnp.exp(ctot) + upd


def wkv7_chunked(r, v, kk, lw, k_dir, b_dir, n_ctx_chunks):
    B, T, da = r.shape
    C, W = WKV_CHUNK, WKV_WIDTH
    n = T // C

    def chunk(d, c):
        back = jnp.where(c < n_ctx_chunks, n_ctx_chunks - 1 - c, n + n_ctx_chunks - 1 - c)
        return jnp.where(d == 0, c, back)

    rows = WKV_ROWS
    shared = pl.BlockSpec((rows, C, da), lambda d, bi, c: (bi, chunk(d, c), 0))
    per_dir = pl.BlockSpec((1, rows, C, da), lambda d, bi, c: (d, bi, chunk(d, c), 0))
    return pl.pallas_call(
        _wkv_chunk_body,
        grid=(2, B // rows, n),
        in_specs=[shared, shared, shared, per_dir, per_dir, per_dir],
        out_specs=per_dir,
        out_shape=jax.ShapeDtypeStruct((2, B, T, da), jnp.float32),
        scratch_shapes=[pltpu.VMEM((rows * (da // W), W, W), jnp.float32)],
        compiler_params=pltpu.CompilerParams(dimension_semantics=("arbitrary",) * 3,
                                             vmem_limit_bytes=VMEM_LIMIT),
    )(r, v, kk, lw, k_dir, b_dir)


def rwkv7_prepare_bm(p, shifted, mu, w0, w_up, a0, a_up, g_up, k_k, k_a):
    p = (p + mu * (shifted - p)).astype(jnp.float32)
    B, T = p.shape[:2]
    cuts = [DA, 2 * DA, 3 * DA, 3 * DA + 2 * DECAY_LORA, 3 * DA + 2 * DECAY_LORA + 2 * AAA_LORA]
    r, k, v, wl, al, gl = jnp.split(p, cuts, axis=-1)
    wl = wl.reshape(B, T, 2, DECAY_LORA)
    al = al.reshape(B, T, 2, AAA_LORA)
    w_log = -jax.nn.softplus(-(w0[:, None, None] + jnp.einsum('btdl,dlc->dbtc', jnp.tanh(wl), w_up))) - 0.5
    lw = -jnp.exp(w_log)
    a = jax.nn.sigmoid(a0[:, None, None] + jnp.einsum('btdl,dlc->dbtc', al, a_up))
    g = jax.nn.sigmoid(gl) @ g_up
    kk = l2norm((k * k_k).reshape(B, T, HA, NA)).reshape(B, T, DA)
    k_dir = k[None] * (1.0 + (a - 1.0) * k_a)
    b_dir = kk[None] * a
    return r, v, kk, g, lw, k_dir, b_dir


def rwkv7_mixer_latent(p_ctx, p_lat, mu, w0, w_up, a0, a_up, g_up, k_k, k_a, r_k, lnx_w, lnx_b):
    t_ctx = p_ctx.shape[1]
    p = jnp.concatenate([p_ctx, p_lat], axis=1)
    shifted = jnp.concatenate([shift_seq(p_ctx), shift_grid(p_lat)], axis=1)
    r, v, kk, g, lw, k_dir, b_dir = rwkv7_prepare_bm(p, shifted, mu, w0, w_up, a0, a_up, g_up, k_k, k_a)
    y2 = wkv7_chunked(r, v, kk, lw, k_dir, b_dir, t_ctx // WKV_CHUNK)
    sl = lambda t: t[..., t_ctx:, :]
    r, v, g, k_dir = sl(r), sl(v), sl(g), sl(k_dir)
    B, T = r.shape[:2]
    y = (sl(y2[0]) + sl(y2[1])).reshape(B, T, HA, NA)
    m = jnp.mean(y, axis=-1, keepdims=True)
    var = jnp.mean(jnp.square(y - m), axis=-1, keepdims=True)
    y = ((y - m) * lax.rsqrt(var + RWKV_EPS)).reshape(B, T, DA) * lnx_w + lnx_b
    heads = lambda t: t.reshape(t.shape[:-1] + (HA, NA))
    bonus = jnp.sum(heads(r)[None] * heads(k_dir) * r_k, axis=(0, 4))
    return (y + (bonus[..., None] * heads(v)).reshape(B, T, DA)) * g


GDN_ROWS = 4


def _gdn_chunk_body(q_ref, k_ref, v_ref, gcx_ref, grow_ref, bx_ref, o_ref, st_ref):
    fwd = pl.program_id(0) == 0

    @pl.when(pl.program_id(2) == 0)
    def _():
        st_ref[...] = jnp.zeros_like(st_ref)

    chains = [_gdn_chunk_math(fwd, q_ref[i], k_ref[i], v_ref[i], gcx_ref[0, i], grow_ref[0, i, 0], bx_ref[0, i],
                              st_ref[i]) for i in range(q_ref.shape[0])]
    for i, (o, st) in enumerate(_interleave(chains)):
        o_ref[0, i] = o
        st_ref[i] = st


def _gdn_chunk_math(fwd, q, k, v, gcx, grow, bx, st):
    C, H, KD = GDN_CHUNK, HB, KB
    W, n = H * KD, H * C
    f32, bf16 = jnp.float32, jnp.bfloat16
    sgn = jnp.where(fwd, 1, -1)
    dot = functools.partial(jnp.dot, preferred_element_type=f32)
    head = lambda x, h: x[:, h * KD:(h + 1) * KD]

    rr = lax.broadcasted_iota(jnp.int32, (n, W), 0)
    cc = lax.broadcasted_iota(jnp.int32, (n, W), 1)
    same_head = (rr // C) == (cc // KD)

    def bdiag(x):
        return jnp.where(same_head, jnp.concatenate([x] * H, axis=0), 0.0).astype(bf16)

    eg = jnp.exp(gcx)
    kb = k * bx
    trans_b = (((1,), (1,)), ((), ()))
    gram = lax.dot_general(jnp.concatenate([bdiag(kb), bdiag(q)], axis=0), bdiag(k), trans_b,
                           preferred_element_type=f32)
    yield

    r2 = lax.broadcasted_iota(jnp.int32, (n, n), 0)
    c2 = lax.broadcasted_iota(jnp.int32, (n, n), 1)
    order = (c2 % C - r2 % C) * sgn
    same2 = (r2 // C) == (c2 // C)
    incl = same2 & (order <= 0)
    strict = same2 & (order < 0)
    g_rows = jnp.concatenate([jnp.concatenate([head(gcx, h)] * (n // KD), axis=1) for h in range(H)], axis=0)
    decay = jnp.exp(jnp.where(incl, g_rows - grow, 0.0))
    a_low = jnp.where(strict, gram[:n] * decay, 0.0)
    attn = jnp.where(incl, gram[n:] * decay, 0.0)

    vb, kbe = v * bx, kb * eg
    x = jnp.concatenate([jnp.concatenate([head(vb, h), head(kbe, h)], axis=1) for h in range(H)], axis=0)
    p = -a_low
    steps = C.bit_length() - 1
    for i in range(steps):
        yield
        pb = p.astype(bf16)
        x = x + dot(pb, x.astype(bf16))
        if i + 1 < steps:
            p = dot(pb, pb)
    u, w = x[:, :KD], x[:, KD:]
    yield

    stb = st.astype(bf16)
    w_bd = jnp.where(same_head, jnp.concatenate([w] * H, axis=1), 0.0).astype(bf16)
    v_new = (u - dot(w_bd, stb)).astype(bf16)
    yield
    o = dot(jnp.concatenate([bdiag(q * eg), attn.astype(bf16)], axis=1), jnp.concatenate([stb, v_new], axis=0))
    yield

    g_last = jnp.where(fwd, gcx[C - 1:C], gcx[0:1])
    trans_a = (((0,), (0,)), ((), ()))
    upd = lax.dot_general(bdiag(k * jnp.exp(g_last - gcx)), v_new, trans_a, preferred_element_type=f32)
    e_last = jnp.exp(g_last)
    st_new = jnp.concatenate([st[h * KD:(h + 1) * KD] * head(e_last, h) for h in range(H)], axis=0) + upd
    return jnp.concatenate([o[h * C:(h + 1) * C] for h in range(H)], axis=1), st_new


def gdn_chunked_pallas(q, k, v, gcx, grow, bx, n_ctx_chunks):
    B, T, W = q.shape
    C = GDN_CHUNK
    n = T // C

    def chunk(d, c):
        back = jnp.where(c < n_ctx_chunks, n_ctx_chunks - 1 - c, n + n_ctx_chunks - 1 - c)
        return jnp.where(d == 0, c, back)

    nb = GDN_ROWS
    shared = pl.BlockSpec((nb, C, W), lambda d, bi, c: (bi, chunk(d, c), 0))
    per_dir = pl.BlockSpec((1, nb, C, W), lambda d, bi, c: (d, bi, chunk(d, c), 0))
    rows = pl.BlockSpec((1, nb, 1, 1, HB * C), lambda d, bi, c: (d, bi, chunk(d, c), 0, 0))
    return pl.pallas_call(
        _gdn_chunk_body,
        grid=(2, B // nb, n),
        in_specs=[shared, shared, shared, per_dir, rows, per_dir],
        out_specs=per_dir,
        out_shape=jax.ShapeDtypeStruct((2, B, T, W), jnp.float32),
        scratch_shapes=[pltpu.VMEM((nb, W, KB), jnp.float32)],
        compiler_params=pltpu.CompilerParams(dimension_semantics=("arbitrary",) * 3,
                                             vmem_limit_bytes=VMEM_LIMIT),
    )(q, k, v, gcx, grow, bx)


def gdn_mixer_latent(p_ctx, p_lat, conv_w, A_log, dt_bias, onorm_g):
    t_ctx = p_ctx.shape[1]
    qc, kc, vc, gc_, bc, _ = gdn_prepare(p_ctx, conv_w, A_log, dt_bias)
    ql, kl, vl, gl_, bl, z = gdn_prepare(p_lat, conv_w, A_log, dt_bias)
    cat = lambda a, b: jnp.concatenate([a, b], axis=1)
    B, T = ql.shape[0], t_ctx + ql.shape[1]
    C = GDN_CHUNK
    flat = lambda t: t.reshape(B, T, DB)
    q, k, v = flat(cat(qc, ql)), flat(cat(kc, kl)), flat(cat(vc, vl))
    g = jnp.moveaxis(cat(gc_, gl_), 2, 0).reshape(2, B, T // C, C, HB)
    gcs = jnp.stack([jnp.cumsum(g[0], axis=2), lax.cumsum(g[1], axis=2, reverse=True)])
    gcx = jnp.repeat(gcs.reshape(2, B, T, HB), KB, axis=-1)
    grow = jnp.swapaxes(gcs, 3, 4).reshape(2, B, T // C, 1, HB * C)
    bx = jnp.repeat(jnp.moveaxis(cat(bc, bl), 2, 0), KB, axis=-1)
    o2 = gdn_chunked_pallas(q, k, v, gcx, grow, bx, t_ctx // C)
    o = (o2[0, :, t_ctx:] + o2[1, :, t_ctx:]).reshape(B, T - t_ctx, HB, KB)
    o = o * lax.rsqrt(jnp.mean(o * o, axis=-1, keepdims=True) + NORM_EPS) * onorm_g
    return o.reshape(B, T - t_ctx, DB) * jax.nn.silu(z.astype(jnp.float32))


ROW_BLOCK = 256
COMBINE_ROWS = 128
LAYOUT_TOKENS = 256
VMEM_LIMIT = 48 * 1024 * 1024


def moe_route(xs, router_w, router_b):
    N = xs.shape[0]
    E = router_w.shape[1]
    scores = jax.nn.sigmoid((xs @ router_w).astype(jnp.float32))
    sel = scores + router_b.astype(jnp.float32)
    grp_score = jnp.sum(lax.top_k(sel.reshape(N, N_GROUPS, E // N_GROUPS), 2)[0], axis=-1)
    top_g = lax.top_k(grp_score, TOPK_GROUPS)[1]
    gmask = jnp.any(top_g[:, :, None] == jnp.arange(N_GROUPS)[None, None, :], axis=1)
    emask = jnp.repeat(gmask, E // N_GROUPS, axis=1)
    top_e = lax.top_k(jnp.where(emask, sel, -jnp.inf), TOP_K)[1]
    wts = jnp.take_along_axis(scores, top_e, axis=1)
    wts = wts / jnp.sum(wts, axis=-1, keepdims=True) * ROUTED_SCALE
    return top_e, wts


def moe_layout(top_e, n_experts):
    N, K = top_e.shape
    nk = N * K
    tb = LAYOUT_TOKENS
    eids = jnp.arange(n_experts, dtype=jnp.int32)
    hit = top_e[:, :, None] == eids[None, None, :]
    used = jnp.any(hit, axis=1).astype(jnp.bfloat16).reshape(N // tb, tb, n_experts)
    before = (jnp.arange(tb)[:, None] > jnp.arange(tb)[None, :]).astype(jnp.bfloat16)
    within = jnp.einsum('ij,bje->bie', before, used, preferred_element_type=jnp.float32)
    blk_tot = jnp.sum(used.astype(jnp.float32), axis=1)
    blk_before = jnp.cumsum(blk_tot, axis=0) - blk_tot
    rank = (within + blk_before[:, None, :]).reshape(N, n_experts).astype(jnp.int32)
    cnt = jnp.sum(blk_tot, axis=0).astype(jnp.int32)
    padded = (cnt + ROW_BLOCK - 1) // ROW_BLOCK * ROW_BLOCK
    pend = jnp.cumsum(padded).astype(jnp.int32)
    row0 = rank + (pend - padded)[None, :]
    pos = jnp.sum(jnp.where(hit, row0[:, None, :], 0), axis=2)
    n_blk = (nk + n_experts * (ROW_BLOCK - 1) + ROW_BLOCK - 1) // ROW_BLOCK
    tok = jnp.broadcast_to(jnp.arange(N, dtype=jnp.int32)[:, None], (N, K))
    row_tok = jnp.zeros((n_blk * ROW_BLOCK,), jnp.int32).at[pos.reshape(nk)].set(tok.reshape(nk))
    blk_start = jnp.arange(n_blk, dtype=jnp.int32) * ROW_BLOCK
    blk_e = jnp.minimum(jnp.searchsorted(pend, blk_start, side='right'), n_experts - 1).astype(jnp.int32)
    n_valid_blk = (pend[-1] // ROW_BLOCK).astype(jnp.int32).reshape(1)
    return blk_e, n_valid_blk, row_tok, pos


def _expert_ffn_body(blk_e_ref, nvb_ref, tok_cur_ref, tok_nxt_ref, x_hbm, w13_ref, w2_ref, y_ref, xbuf, sem):
    i = pl.program_id(0)
    nvb = nvb_ref[0]
    slot = lax.rem(i, 2)
    de = w2_ref.shape[1]

    def row_copy(tok_ref, r, s):
        t = tok_ref[0, 0, r]
        return pltpu.make_async_copy(x_hbm.at[pl.ds(t, 1)], xbuf.at[s, pl.ds(r, 1)], sem.at[s])

    def start_gather(tok_ref, s):
        for r in range(ROW_BLOCK):
            row_copy(tok_ref, r, s).start()

    @pl.when((i == 0) & (nvb > 0))
    def _():
        start_gather(tok_cur_ref, 0)

    @pl.when(i + 1 < nvb)
    def _():
        start_gather(tok_nxt_ref, 1 - slot)

    @pl.when(i < nvb)
    def _():
        pltpu.make_async_copy(x_hbm.at[pl.ds(0, ROW_BLOCK)], xbuf.at[slot], sem.at[slot]).wait()
        xb = xbuf[slot].astype(jnp.bfloat16)
        h13 = jnp.dot(xb, w13_ref[0], preferred_element_type=jnp.float32)
        h = (jax.nn.silu(h13[:, :de]) * h13[:, de:]).astype(jnp.bfloat16)
        y_ref[...] = jnp.dot(h, w2_ref[0], preferred_element_type=jnp.float32)

    @pl.when(i >= nvb)
    def _():
        y_ref[...] = jnp.zeros_like(y_ref)


def expert_ffn_rows(xs, blk_e, n_valid_blk, row_tok, w13, w2):
    N, D = xs.shape
    n_blk = blk_e.shape[0]
    de2 = w13.shape[2]
    tok3 = row_tok.reshape(n_blk, 1, ROW_BLOCK)
    last = n_blk - 1
    grid_spec = pltpu.PrefetchScalarGridSpec(
        num_scalar_prefetch=2,
        grid=(n_blk,),
        in_specs=[
            pl.BlockSpec((1, 1, ROW_BLOCK), lambda i, be, nv: (i, 0, 0), memory_space=pltpu.SMEM),
            pl.BlockSpec((1, 1, ROW_BLOCK), lambda i, be, nv: (jnp.minimum(i + 1, last), 0, 0),
                         memory_space=pltpu.SMEM),
            pl.BlockSpec(memory_space=pl.ANY),
            pl.BlockSpec((1, D, de2), lambda i, be, nv: (be[i], 0, 0)),
            pl.BlockSpec((1, de2 // 2, D), lambda i, be, nv: (be[i], 0, 0)),
        ],
        out_specs=pl.BlockSpec((ROW_BLOCK, D), lambda i, be, nv: (i, 0)),
        scratch_shapes=[pltpu.VMEM((2, ROW_BLOCK, D), jnp.float32), pltpu.SemaphoreType.DMA((2,))],
    )
    return pl.pallas_call(
        _expert_ffn_body,
        grid_spec=grid_spec,
        out_shape=jax.ShapeDtypeStruct((n_blk * ROW_BLOCK, D), jnp.float32),
        compiler_params=pltpu.CompilerParams(dimension_semantics=("arbitrary",), vmem_limit_bytes=VMEM_LIMIT),
    )(blk_e, n_valid_blk, tok3, tok3, xs, w13, w2)


def _combine_body(pos_cur_ref, pos_nxt_ref, y_hbm, wts_ref, hin_ref, hres_ref, gt_ref, sw13_ref, sw2_ref,
                  fg_ref, out_ref, ybuf, sem):
    j = pl.program_id(0)
    nt = pl.num_programs(0)
    slot = lax.rem(j, 2)
    K = wts_ref.shape[1]
    tm = wts_ref.shape[0]
    ds = sw2_ref.shape[0]

    def row_copy(pos_ref, k, r, s):
        p = pos_ref[0, 0, k * tm + r]
        return pltpu.make_async_copy(y_hbm.at[pl.ds(p, 1)], ybuf.at[s, k, pl.ds(r, 1)], sem.at[s])

    def start_gather(pos_ref, s):
        for k in range(K):
            for r in range(tm):
                row_copy(pos_ref, k, r, s).start()

    @pl.when(j == 0)
    def _():
        start_gather(pos_cur_ref, 0)

    @pl.when(j + 1 < nt)
    def _():
        start_gather(pos_nxt_ref, 1 - slot)

    for k in range(K):
        pltpu.make_async_copy(y_hbm.at[pl.ds(0, tm)], ybuf.at[slot, k], sem.at[slot]).wait()

    w = wts_ref[...]
    acc = ybuf[slot, 0] * w[:, 0:1]
    for k in range(1, K):
        acc = acc + ybuf[slot, k] * w[:, k:k + 1]
    hb = hin_ref[...].astype(jnp.bfloat16)
    s13 = jnp.dot(hb, sw13_ref[...], preferred_element_type=jnp.float32)
    sh = (jax.nn.silu(s13[:, :ds]) * s13[:, ds:]).astype(jnp.bfloat16)
    acc = acc + jnp.dot(sh, sw2_ref[...], preferred_element_type=jnp.float32)
    h = hres_ref[...] + gt_ref[0] * acc
    ms = jnp.mean(h * h, axis=-1, keepdims=True)
    out_ref[...] = h * lax.rsqrt(ms + NORM_EPS) * fg_ref[...]


def moe_combine_norm(y_rows, pos, wts, h_in, h_res, gate, sw13, sw2, final_g, seq):
    N, D = h_in.shape
    K = pos.shape[1]
    tm = COMBINE_ROWS
    nt = N // tm
    tiles_per_seq = seq // tm
    pos3 = pos.reshape(nt, tm, K).transpose(0, 2, 1).reshape(nt, 1, K * tm)
    last = nt - 1
    return pl.pallas_call(
        _combine_body,
        grid=(nt,),
        in_specs=[
            pl.BlockSpec((1, 1, K * tm), lambda j: (j, 0, 0), memory_space=pltpu.SMEM),
            pl.BlockSpec((1, 1, K * tm), lambda j: (jnp.minimum(j + 1, last), 0, 0), memory_space=pltpu.SMEM),
            pl.BlockSpec(memory_space=pl.ANY),
            pl.BlockSpec((tm, K), lambda j: (j, 0)),
            pl.BlockSpec((tm, D), lambda j: (j, 0)),
            pl.BlockSpec((tm, D), lambda j: (j, 0)),
            pl.BlockSpec((1, 1, D), lambda j: (j // tiles_per_seq, 0, 0)),
            pl.BlockSpec(sw13.shape, lambda j: (0, 0)),
            pl.BlockSpec(sw2.shape, lambda j: (0, 0)),
            pl.BlockSpec((1, D), lambda j: (0, 0)),
        ],
        out_specs=pl.BlockSpec((tm, D), lambda j: (j, 0)),
        out_shape=jax.ShapeDtypeStruct((N, D), jnp.float32),
        scratch_shapes=[pltpu.VMEM((2, K, tm, D), jnp.float32), pltpu.SemaphoreType.DMA((2,))],
        compiler_params=pltpu.CompilerParams(dimension_semantics=("arbitrary",), vmem_limit_bytes=VMEM_LIMIT),
    )(pos3, pos3, y_rows, wts, h_in, h_res, gate.reshape(-1, 1, D), sw13, sw2, final_g.reshape(1, D))


def moe_block_final(h_in, h_res, gate, router_w, router_b, w1, w3, w2, sw1, sw3, sw2, final_g, seq):
    bf = jnp.bfloat16
    top_e, wts = moe_route(h_in, router_w, router_b)
    blk_e, n_valid_blk, row_tok, pos = moe_layout(top_e, router_w.shape[1])
    w13 = jnp.concatenate([w1, w3], axis=-1).astype(bf)
    y_rows = expert_ffn_rows(h_in, blk_e, n_valid_blk, row_tok, w13, w2.astype(bf))
    sw13 = jnp.concatenate([sw1, sw3], axis=-1).astype(bf)
    return moe_combine_norm(y_rows, pos, wts, h_in, h_res, gate, sw13, sw2.astype(bf), final_g, seq)


def kernel(x, c, ctx, c_ctx, w_ada, b_ada, norm1_g, w_in, mu_shift, w0, w_up, a0, a_up, g_up,
           k_k, k_a, r_k, lnx_w, lnx_b, conv_w, A_log, dt_bias, onorm_g, w_out, norm2_g,
           router_w, router_b, exp_w1, exp_w3, exp_w2, sh_w1, sh_w3, sh_w2, final_g):
    h_x, h_c = x, ctx
    l = 0
    mod = jax.nn.silu(c) @ w_ada[l] + b_ada[l]
    mod_c = jax.nn.silu(c_ctx) @ w_ada[l] + b_ada[l]
    sh1, sc1, gt1, sh2, sc2, gt2 = (m[:, None] for m in jnp.split(mod, 6, axis=-1))
    sh1c, sc1c, gt1c, sh2c, sc2c, gt2c = jnp.split(mod_c, 6, axis=-1)

    p_x = (rmsnorm(h_x, norm1_g[l]) * (1.0 + sc1) + sh1) @ w_in[l]
    p_c = (rmsnorm(h_c, norm1_g[l]) * (1.0 + sc1c) + sh1c) @ w_in[l]
    ya_x = rwkv7_mixer_latent(p_c[..., :RWKV_COLS], p_x[..., :RWKV_COLS], mu_shift[l], w0[l], w_up[l],
                              a0[l], a_up[l], g_up[l], k_k[l], k_a[l], r_k[l], lnx_w[l], lnx_b[l])
    yb_x = gdn_mixer_latent(p_c[..., RWKV_COLS:], p_x[..., RWKV_COLS:], conv_w[l], A_log[l],
                            dt_bias[l], onorm_g[l])
    h_x = h_x + gt1 * (jnp.concatenate([ya_x, yb_x], axis=-1).astype(x.dtype) @ w_out[l])
    B, T, D = h_x.shape
    h_in = (rmsnorm(h_x, norm2_g[l]) * (1.0 + sc2) + sh2).reshape(B * T, D)
    out = moe_block_final(h_in, h_x.reshape(B * T, D), gt2[:, 0], router_w[l], router_b[l],
                          exp_w1[l], exp_w3[l], exp_w2[l], sh_w1[l], sh_w3[l], sh_w2[l], final_g, T)
    return out.reshape(B, T, D)
```

```python
import functools

import jax
import jax.numpy as jnp
from jax import lax
from jax.experimental import pallas as pl
from jax.experimental.pallas import tpu as pltpu

D_MODEL = 1024
BATCH = 32
SEQ = 2048
DEPTH = 1
CTX_LEN = 256
GRID_W = 64
D_MIX = D_MODEL
DA = D_MIX // 2
NA = 64
HA = DA // NA
DECAY_LORA = 64
AAA_LORA = 64
GATE_LORA = 128
RWKV_COLS = 3 * DA + 2 * DECAY_LORA + 2 * AAA_LORA + GATE_LORA
RWKV_EPS = 64e-5
DB = D_MIX - DA
HB = 4
KB = DB // HB
GDN_CONV = 5
GDN_CHUNK = 64
GDN_COLS = 4 * DB + 4 * HB
IN_COLS = RWKV_COLS + GDN_COLS
N_EXPERTS = 256
TOP_K = 8
N_GROUPS = 8
TOPK_GROUPS = 4
D_EXPERT = 256
D_SHARED = 256
ROUTED_SCALE = 2.5
MOE_BLOCK = 256
NORM_EPS = 1e-6


def rmsnorm(x, g):
    xf = x.astype(jnp.float32)
    y = xf * lax.rsqrt(jnp.mean(xf * xf, axis=-1, keepdims=True) + NORM_EPS)
    return (y * g.astype(jnp.float32)).astype(x.dtype)


def l2norm(x):
    xf = x.astype(jnp.float32)
    return xf * lax.rsqrt(jnp.sum(xf * xf, axis=-1, keepdims=True) + 1e-6)


def shift_seq(p):
    h = p.shape[-1] // 2
    prev = jnp.pad(p[:, :-1, :h], ((0, 0), (1, 0), (0, 0)))
    nxt = jnp.pad(p[:, 1:, h:], ((0, 0), (0, 1), (0, 0)))
    return jnp.concatenate([prev, nxt], axis=-1)


def shift_grid(p):
    B, T, C = p.shape
    rows = T // GRID_W
    q = C // 4
    g = p.reshape(B, rows, GRID_W, C)
    left = jnp.pad(g[:, :, :-1, :q], ((0, 0), (0, 0), (1, 0), (0, 0)))
    right = jnp.pad(g[:, :, 1:, q:2 * q], ((0, 0), (0, 0), (0, 1), (0, 0)))
    up = jnp.pad(g[:, :-1, :, 2 * q:3 * q], ((0, 0), (1, 0), (0, 0), (0, 0)))
    down = jnp.pad(g[:, 1:, :, 3 * q:], ((0, 0), (0, 1), (0, 0), (0, 0)))
    return jnp.concatenate([left, right, up, down], axis=-1).reshape(B, T, C)


def dwconv_centered(x, w):
    C = x.shape[-1]
    return lax.conv_general_dilated(
        x, w[:, None, :].astype(x.dtype), window_strides=(1,),
        padding=[(GDN_CONV // 2, GDN_CONV // 2)],
        dimension_numbers=('NWC', 'WIO', 'NWC'), feature_group_count=C)


def gdn_prepare(p, conv_w, A_log, dt_bias):
    B, T, _ = p.shape
    qkv = jax.nn.silu(dwconv_centered(p[..., :3 * DB], conv_w)).astype(jnp.float32)
    q, k, v = jnp.split(qkv, 3, axis=-1)
    q = l2norm(q.reshape(B, T, HB, KB)) * (KB ** -0.5)
    k = l2norm(k.reshape(B, T, HB, KB))
    v = v.reshape(B, T, HB, KB)
    z = p[..., 3 * DB:4 * DB]
    gl = p[..., 4 * DB:].astype(jnp.float32).reshape(B, T, 2, 2, HB)
    g = -jnp.exp(A_log) * jax.nn.softplus(gl[:, :, 0] + dt_bias)
    beta = jax.nn.sigmoid(gl[:, :, 1])
    return q, k, v, g, beta, z


WKV_CHUNK = 64
WKV_GROUP = 4
WKV_WIDTH = WKV_GROUP * NA
WKV_ROWS = 2


def _wkv_chunk_body(r_ref, v_ref, kk_ref, lw_ref, k_ref, b_ref, y_ref, st_ref):
    W = WKV_WIDTH
    fwd = pl.program_id(0) == 0

    @pl.when(pl.program_id(2) == 0)
    def _():
        st_ref[...] = jnp.zeros_like(st_ref)

    rows, n_groups = r_ref.shape[0], r_ref.shape[2] // W
    where = [(i, slice(g * W, (g + 1) * W)) for i in range(rows) for g in range(n_groups)]
    chains = [_wkv_chunk_math(fwd, r_ref[i, :, lanes], v_ref[i, :, lanes], kk_ref[i, :, lanes],
                              lw_ref[0, i, :, lanes], k_ref[0, i, :, lanes], b_ref[0, i, :, lanes], st_ref[s])
              for s, (i, lanes) in enumerate(where)]
    for s, (y, st) in enumerate(_interleave(chains)):
        i, lanes = where[s]
        y_ref[0, i, :, lanes] = y
        st_ref[s] = st


def _interleave(chains):
    results = [None] * len(chains)
    live = list(range(len(chains)))
    while live:
        for i in list(live):
            try:
                next(chains[i])
            except StopIteration as stop:
                results[i] = stop.value
                live.remove(i)
    return results


def _wkv_chunk_math(fwd, r, v, kk, lw, k, b, st):
    C, W, G = WKV_CHUNK, WKV_WIDTH, WKV_GROUP
    f32, bf16 = jnp.float32, jnp.bfloat16
    sgn = jnp.where(fwd, 1, -1)

    row = lax.broadcasted_iota(jnp.int32, (C, C), 0)
    col = lax.broadcasted_iota(jnp.int32, (C, C), 1)
    tri = jnp.where((col - row) * sgn <= 0, 1.0, 0.0).astype(bf16)
    hi = lw.astype(bf16)
    rem = lw - hi.astype(f32)
    mid = rem.astype(bf16)
    lo = (rem - mid.astype(f32)).astype(bf16)
    dot = functools.partial(jnp.dot, preferred_element_type=f32)
    cinc = dot(tri, hi) + dot(tri, mid) + dot(tri, lo)
    yield
    ctot = jnp.where(fwd, cinc[C - 1:C], cinc[0:1])
    e_out = jnp.exp(-cinc)
    e_last = jnp.exp(ctot - cinc)

    rr = lax.broadcasted_iota(jnp.int32, (G * C, W), 0)
    cc = lax.broadcasted_iota(jnp.int32, (G * C, W), 1)
    same_head = (rr // C) == (cc // NA)

    def bdiag(x):
        return jnp.where(same_head, jnp.concatenate([x] * G, axis=0), 0.0).astype(bf16)

    lhs = jnp.concatenate([bdiag(kk * jnp.exp(cinc - lw)), bdiag(r * jnp.exp(cinc))], axis=0)
    rhs = jnp.concatenate([bdiag(b * e_out), bdiag(k * e_out)], axis=0)
    trans_b = (((1,), (1,)), ((), ()))
    gram = lax.dot_general(lhs, rhs, trans_b, preferred_element_type=f32)
    hs = lax.dot_general(lhs, st.astype(bf16), trans_b, preferred_element_type=f32)
    yield

    tr, tc = rr % C, cc % C
    order = (tc - tr) * sgn
    strict = order < 0
    incl = order <= 0
    n = G * C
    a_bk = jnp.where(strict, gram[:n, :n], 0.0)
    a_kk = jnp.where(strict, gram[:n, n:], 0.0)
    rb = jnp.where(incl, gram[n:, :n], 0.0)
    rk = jnp.where(incl, gram[n:, n:], 0.0)

    vbd = bdiag(v)
    x = hs[:n] + dot(a_kk.astype(bf16), vbd)
    p = -a_bk
    steps = C.bit_length() - 1
    for i in range(steps):
        yield
        pb = p.astype(bf16)
        x = x + dot(pb, x.astype(bf16))
        if i + 1 < steps:
            p = dot(pb, pb)
    ub = x.astype(bf16)
    yield

    y = hs[n:] + dot(jnp.concatenate([rk, -rb], axis=1).astype(bf16), jnp.concatenate([vbd, ub], axis=0))
    yield

    trans_a = (((0,), (0,)), ((), ()))
    upd = lax.dot_general(jnp.concatenate([vbd, -ub], axis=0),
                          jnp.concatenate([bdiag(k * e_last), bdiag(b * e_last)], axis=0),
                          trans_a, preferred_element_type=f32)
    return sum(y[h * C:(h + 1) * C] for h in range(G)), st * jnp.exp(ctot) + upd


def wkv7_chunked(r, v, kk, lw, k_dir, b_dir, n_ctx_chunks):
    B, T, da = r.shape
    C, W = WKV_CHUNK, WKV_WIDTH
    n = T // C

    def chunk(d, c):
        back = jnp.where(c < n_ctx_chunks, n_ctx_chunks - 1 - c, n + n_ctx_chunks - 1 - c)
        return jnp.where(d == 0, c, back)

    rows = WKV_ROWS
    shared = pl.BlockSpec((rows, C, da), lambda d, bi, c: (bi, chunk(d, c), 0))
    per_dir = pl.BlockSpec((1, rows, C, da), lambda d, bi, c: (d, bi, chunk(d, c), 0))
    return pl.pallas_call(
        _wkv_chunk_body,
        grid=(2, B // rows, n),
        in_specs=[shared, shared, shared, per_dir, per_dir, per_dir],
        out_specs=per_dir,
        out_shape=jax.ShapeDtypeStruct((2, B, T, da), jnp.float32),
        scratch_shapes=[pltpu.VMEM((rows * (da // W), W, W), jnp.float32)],
        compiler_params=pltpu.CompilerParams(dimension_semantics=("arbitrary",) * 3,
                                             vmem_limit_bytes=VMEM_LIMIT),
    )(r, v, kk, lw, k_dir, b_dir)


def rwkv7_prepare_bm(p, shifted, mu, w0, w_up, a0, a_up, g_up, k_k, k_a):
    p = (p + mu * (shifted - p)).astype(jnp.float32)
    B, T = p.shape[:2]
    cuts = [DA, 2 * DA, 3 * DA, 3 * DA + 2 * DECAY_LORA, 3 * DA + 2 * DECAY_LORA + 2 * AAA_LORA]
    r, k, v, wl, al, gl = jnp.split(p, cuts, axis=-1)
    wl = wl.reshape(B, T, 2, DECAY_LORA)
    al = al.reshape(B, T, 2, AAA_LORA)
    w_log = -jax.nn.softplus(-(w0[:, None, None] + jnp.einsum('btdl,dlc->dbtc', jnp.tanh(wl), w_up))) - 0.5
    lw = -jnp.exp(w_log)
    a = jax.nn.sigmoid(a0[:, None, None] + jnp.einsum('btdl,dlc->dbtc', al, a_up))
    g = jax.nn.sigmoid(gl) @ g_up
    kk = l2norm((k * k_k).reshape(B, T, HA, NA)).reshape(B, T, DA)
    k_dir = k[None] * (1.0 + (a - 1.0) * k_a)
    b_dir = kk[None] * a
    return r, v, kk, g, lw, k_dir, b_dir


def rwkv7_mixer_latent(p_ctx, p_lat, mu, w0, w_up, a0, a_up, g_up, k_k, k_a, r_k, lnx_w, lnx_b):
    t_ctx = p_ctx.shape[1]
    p = jnp.concatenate([p_ctx, p_lat], axis=1)
    shifted = jnp.concatenate([shift_seq(p_ctx), shift_grid(p_lat)], axis=1)
    r, v, kk, g, lw, k_dir, b_dir = rwkv7_prepare_bm(p, shifted, mu, w0, w_up, a0, a_up, g_up, k_k, k_a)
    y2 = wkv7_chunked(r, v, kk, lw, k_dir, b_dir, t_ctx // WKV_CHUNK)
    sl = lambda t: t[..., t_ctx:, :]
    r, v, g, k_dir = sl(r), sl(v), sl(g), sl(k_dir)
    B, T = r.shape[:2]
    y = (sl(y2[0]) + sl(y2[1])).reshape(B, T, HA, NA)
    m = jnp.mean(y, axis=-1, keepdims=True)
    var = jnp.mean(jnp.square(y - m), axis=-1, keepdims=True)
    y = ((y - m) * lax.rsqrt(var + RWKV_EPS)).reshape(B, T, DA) * lnx_w + lnx_b
    heads = lambda t: t.reshape(t.shape[:-1] + (HA, NA))
    bonus = jnp.sum(heads(r)[None] * heads(k_dir) * r_k, axis=(0, 4))
    return (y + (bonus[..., None] * heads(v)).reshape(B, T, DA)) * g


GDN_ROWS = 4


def _gdn_chunk_body(q_ref, k_ref, v_ref, gc_ref, grow_ref, beta_ref, o_ref, st_ref):
    fwd = pl.program_id(0) == 0

    @pl.when(pl.program_id(2) == 0)
    def _():
        st_ref[...] = jnp.zeros_like(st_ref)

    def over_lanes(a):
        return jnp.concatenate([jnp.broadcast_to(a[:, h:h + 1], (a.shape[0], KB)) for h in range(HB)], axis=1)

    chains = [_gdn_chunk_math(fwd, q_ref[i], k_ref[i], v_ref[i], over_lanes(gc_ref[0, i]), grow_ref[0, i, 0],
                              over_lanes(beta_ref[0, i]), st_ref[i]) for i in range(q_ref.shape[0])]
    for i, (o, st) in enumerate(_interleave(chains)):
        o_ref[0, i] = o
        st_ref[i] = st


def _gdn_chunk_math(fwd, q, k, v, gcx, grow, bx, st):
    C, H, KD = GDN_CHUNK, HB, KB
    W, n = H * KD, H * C
    f32, bf16 = jnp.float32, jnp.bfloat16
    sgn = jnp.where(fwd, 1, -1)
    dot = functools.partial(jnp.dot, preferred_element_type=f32)
    head = lambda x, h: x[:, h * KD:(h + 1) * KD]

    rr = lax.broadcasted_iota(jnp.int32, (n, W), 0)
    cc = lax.broadcasted_iota(jnp.int32, (n, W), 1)
    same_head = (rr // C) == (cc // KD)

    def bdiag(x):
        return jnp.where(same_head, jnp.concatenate([x] * H, axis=0), 0.0).astype(bf16)

    eg = jnp.exp(gcx)
    kb = k * bx
    trans_b = (((1,), (1,)), ((), ()))
    gram = lax.dot_general(jnp.concatenate([bdiag(kb), bdiag(q)], axis=0), bdiag(k), trans_b,
                           preferred_element_type=f32)
    yield

    r2 = lax.broadcasted_iota(jnp.int32, (n, n), 0)
    c2 = lax.broadcasted_iota(jnp.int32, (n, n), 1)
    order = (c2 % C - r2 % C) * sgn
    same2 = (r2 // C) == (c2 // C)
    incl = same2 & (order <= 0)
    strict = same2 & (order < 0)
    g_rows = jnp.concatenate([jnp.concatenate([head(gcx, h)] * (n // KD), axis=1) for h in range(H)], axis=0)
    decay = jnp.exp(jnp.where(incl, g_rows - grow, 0.0))
    a_low = jnp.where(strict, gram[:n] * decay, 0.0)
    attn = jnp.where(incl, gram[n:] * decay, 0.0)

    vb, kbe = v * bx, kb * eg
    x = jnp.concatenate([jnp.concatenate([head(vb, h), head(kbe, h)], axis=1) for h in range(H)], axis=0)
    p = -a_low
    steps = C.bit_length() - 1
    for i in range(steps):
        yield
        pb = p.astype(bf16)
        x = x + dot(pb, x.astype(bf16))
        if i + 1 < steps:
            p = dot(pb, pb)
    u, w = x[:, :KD], x[:, KD:]
    yield

    stb = st.astype(bf16)
    w_bd = jnp.where(same_head, jnp.concatenate([w] * H, axis=1), 0.0).astype(bf16)
    v_new = (u - dot(w_bd, stb)).astype(bf16)
    yield
    o = dot(jnp.concatenate([bdiag(q * eg), attn.astype(bf16)], axis=1), jnp.concatenate([stb, v_new], axis=0))
    yield

    g_last = jnp.where(fwd, gcx[C - 1:C], gcx[0:1])
    trans_a = (((0,), (0,)), ((), ()))
    upd = lax.dot_general(bdiag(k * jnp.exp(g_last - gcx)), v_new, trans_a, preferred_element_type=f32)
    e_last = jnp.exp(g_last)
    st_new = jnp.concatenate([st[h * KD:(h + 1) * KD] * head(e_last, h) for h in range(H)], axis=0) + upd
    return jnp.concatenate([o[h * C:(h + 1) * C] for h in range(H)], axis=1), st_new


def gdn_chunked_pallas(q, k, v, gc, grow, beta, n_ctx_chunks):
    B, T, W = q.shape
    C = GDN_CHUNK
    n = T // C

    def chunk(d, c):
        back = jnp.where(c < n_ctx_chunks, n_ctx_chunks - 1 - c, n + n_ctx_chunks - 1 - c)
        return jnp.where(d == 0, c, back)

    nb = GDN_ROWS
    shared = pl.BlockSpec((nb, C, W), lambda d, bi, c: (bi, chunk(d, c), 0))
    per_dir = pl.BlockSpec((1, nb, C, W), lambda d, bi, c: (d, bi, chunk(d, c), 0))
    rows = pl.BlockSpec((1, nb, 1, 1, HB * C), lambda d, bi, c: (d, bi, chunk(d, c), 0, 0))
    per_head = pl.BlockSpec((1, nb, C, HB), lambda d, bi, c: (d, bi, chunk(d, c), 0))
    return pl.pallas_call(
        _gdn_chunk_body,
        grid=(2, B // nb, n),
        in_specs=[shared, shared, shared, per_head, rows, per_head],
        out_specs=per_dir,
        out_shape=jax.ShapeDtypeStruct((2, B, T, W), jnp.float32),
        scratch_shapes=[pltpu.VMEM((nb, W, KB), jnp.float32)],
        compiler_params=pltpu.CompilerParams(dimension_semantics=("arbitrary",) * 3,
                                             vmem_limit_bytes=VMEM_LIMIT),
    )(q, k, v, gc, grow, beta)


def gdn_mixer_latent(p_ctx, p_lat, conv_w, A_log, dt_bias, onorm_g):
    t_ctx = p_ctx.shape[1]
    qc, kc, vc, gc_, bc, _ = gdn_prepare(p_ctx, conv_w, A_log, dt_bias)
    ql, kl, vl, gl_, bl, z = gdn_prepare(p_lat, conv_w, A_log, dt_bias)
    cat = lambda a, b: jnp.concatenate([a, b], axis=1)
    B, T = ql.shape[0], t_ctx + ql.shape[1]
    C = GDN_CHUNK
    flat = lambda t: t.reshape(B, T, DB)
    q, k, v = flat(cat(qc, ql)), flat(cat(kc, kl)), flat(cat(vc, vl))
    g = jnp.moveaxis(cat(gc_, gl_), 2, 0).reshape(2, B, T // C, C, HB)
    gcs = jnp.stack([jnp.cumsum(g[0], axis=2), lax.cumsum(g[1], axis=2, reverse=True)])
    grow = jnp.swapaxes(gcs, 3, 4).reshape(2, B, T // C, 1, HB * C)
    beta = jnp.moveaxis(cat(bc, bl), 2, 0)
    o2 = gdn_chunked_pallas(q, k, v, gcs.reshape(2, B, T, HB), grow, beta, t_ctx // C)
    o = (o2[0, :, t_ctx:] + o2[1, :, t_ctx:]).reshape(B, T - t_ctx, HB, KB)
    o = o * lax.rsqrt(jnp.mean(o * o, axis=-1, keepdims=True) + NORM_EPS) * onorm_g
    return o.reshape(B, T - t_ctx, DB) * jax.nn.silu(z.astype(jnp.float32))


LANES = 128
ROW_BLOCK = 256
COMBINE_ROWS = 128
LAYOUT_TOKENS = 256
VMEM_LIMIT = 48 * 1024 * 1024


def moe_route(xs, router_w, router_b):
    N = xs.shape[0]
    E = router_w.shape[1]
    scores = jax.nn.sigmoid((xs @ router_w).astype(jnp.float32))
    sel = scores + router_b.astype(jnp.float32)
    grp_score = jnp.sum(lax.top_k(sel.reshape(N, N_GROUPS, E // N_GROUPS), 2)[0], axis=-1)
    top_g = lax.top_k(grp_score, TOPK_GROUPS)[1]
    gmask = jnp.any(top_g[:, :, None] == jnp.arange(N_GROUPS)[None, None, :], axis=1)
    emask = jnp.repeat(gmask, E // N_GROUPS, axis=1)
    top_e = lax.top_k(jnp.where(emask, sel, -jnp.inf), TOP_K)[1]
    wts = jnp.take_along_axis(scores, top_e, axis=1)
    wts = wts / jnp.sum(wts, axis=-1, keepdims=True) * ROUTED_SCALE
    return top_e, wts


MID_ROWS = 256


def _first_max(x, idx, axis_len):
    m = jnp.max(x, axis=0, keepdims=True)
    i = jnp.min(jnp.where(x == m, idx, axis_len), axis=0, keepdims=True)
    return m, i


def _mid_body(ya_ref, yb_ref, x_ref, gt1_ref, sc2_ref, sh2_ref, wo_ref, g2_ref, rwt_ref, rb_ref,
              hx_ref, hin_ref, slab_ref, te_ref, wt_ref):
    f32, bf16 = jnp.float32, jnp.bfloat16
    half = ya_ref.shape[1]
    mix = (jnp.dot(ya_ref[...].astype(bf16), wo_ref[:half], preferred_element_type=f32)
           + jnp.dot(yb_ref[...].astype(bf16), wo_ref[half:], preferred_element_type=f32))
    hx = x_ref[...] + gt1_ref[0] * mix
    hx_ref[...] = hx
    ms = jnp.mean(hx * hx, axis=-1, keepdims=True)
    hin = hx * lax.rsqrt(ms + NORM_EPS) * g2_ref[...] * (1.0 + sc2_ref[0]) + sh2_ref[0]
    hin_ref[...] = hin
    lanes = slab_ref.shape[1]
    sub = hin.shape[1] // lanes
    for j in range(sub):
        slab_ref[pl.ds(j, hin.shape[0], stride=sub), :] = hin[:, j * lanes:(j + 1) * lanes]

    logits = lax.dot_general(rwt_ref[...], hin.astype(bf16), (((1,), (1,)), ((), ())),
                             preferred_element_type=f32)
    n_e, tm = logits.shape
    per_group = n_e // N_GROUPS
    scores = jax.nn.sigmoid(logits)
    sel = scores + rb_ref[...]
    neg = -jnp.inf

    lidx = lax.broadcasted_iota(jnp.int32, (per_group, tm), 0)
    gs = []
    for g in range(N_GROUPS):
        blk = sel[g * per_group:(g + 1) * per_group]
        m1, i1 = _first_max(blk, lidx, per_group)
        m2 = jnp.max(jnp.where(lidx == i1, neg, blk), axis=0, keepdims=True)
        gs.append(m1 + m2)
    gsc = jnp.concatenate(gs, axis=0)

    gidx = lax.broadcasted_iota(jnp.int32, (N_GROUPS, tm), 0)
    chosen = jnp.zeros((N_GROUPS, tm), f32)
    for _ in range(TOPK_GROUPS):
        _, gi = _first_max(gsc, gidx, N_GROUPS)
        hit = gidx == gi
        chosen = jnp.where(hit, 1.0, chosen)
        gsc = jnp.where(hit, neg, gsc)
    allowed = jnp.concatenate([jnp.broadcast_to(chosen[g:g + 1], (per_group, tm)) for g in range(N_GROUPS)], axis=0)
    masked = jnp.where(allowed > 0.5, sel, neg)

    eidx = lax.broadcasted_iota(jnp.int32, (n_e, tm), 0)
    ids, ws = [], []
    for _ in range(TOP_K):
        _, ei = _first_max(masked, eidx, n_e)
        hit = eidx == ei
        ids.append(ei)
        ws.append(jnp.sum(jnp.where(hit, scores, 0.0), axis=0, keepdims=True))
        masked = jnp.where(hit, neg, masked)
    w = jnp.concatenate(ws, axis=0)
    te_ref[...] = jnp.concatenate(ids, axis=0)
    wt_ref[...] = w / jnp.sum(w, axis=0, keepdims=True) * ROUTED_SCALE


def mixer_out_route(ya, yb, x, gt1, sc2, sh2, w_out, norm2_g, router_w, router_b, seq):
    N, D = x.shape
    sub = D // LANES
    E = router_w.shape[1]
    tm = MID_ROWS
    per_seq = seq // tm
    row = lambda i: (i, 0)
    per_batch = pl.BlockSpec((1, 1, D), lambda i: (i // per_seq, 0, 0))
    whole = lambda a: pl.BlockSpec(a.shape, lambda i: (0,) * a.ndim)
    wo = w_out.astype(jnp.bfloat16)
    rwt = router_w.T.astype(jnp.bfloat16)
    g2 = norm2_g.reshape(1, D)
    rb = router_b.reshape(E, 1).astype(jnp.float32)
    b3 = lambda a: a.reshape(-1, 1, D)
    return pl.pallas_call(
        _mid_body,
        grid=(N // tm,),
        in_specs=[pl.BlockSpec((tm, D // 2), row), pl.BlockSpec((tm, D // 2), row), pl.BlockSpec((tm, D), row),
                  per_batch, per_batch, per_batch, whole(wo), whole(g2), whole(rwt), whole(rb)],
        out_specs=[pl.BlockSpec((tm, D), row), pl.BlockSpec((tm, D), row), pl.BlockSpec((tm * sub, LANES), row),
                   pl.BlockSpec((TOP_K, tm), lambda i: (0, i)), pl.BlockSpec((TOP_K, tm), lambda i: (0, i))],
        out_shape=[jax.ShapeDtypeStruct((N, D), jnp.float32), jax.ShapeDtypeStruct((N, D), jnp.float32),
                   jax.ShapeDtypeStruct((N * sub, LANES), jnp.float32),
                   jax.ShapeDtypeStruct((TOP_K, N), jnp.int32), jax.ShapeDtypeStruct((TOP_K, N), jnp.float32)],
        compiler_params=pltpu.CompilerParams(dimension_semantics=("arbitrary",), vmem_limit_bytes=VMEM_LIMIT),
    )(ya, yb, x, b3(gt1), b3(sc2), b3(sh2), wo, g2, rwt, rb)


def moe_layout(top_e, n_experts):
    N, K = top_e.shape
    nk = N * K
    tb = LAYOUT_TOKENS
    eids = jnp.arange(n_experts, dtype=jnp.int32)
    hit = top_e[:, :, None] == eids[None, None, :]
    used = jnp.any(hit, axis=1).astype(jnp.bfloat16).reshape(N // tb, tb, n_experts)
    before = (jnp.arange(tb)[:, None] > jnp.arange(tb)[None, :]).astype(jnp.bfloat16)
    within = jnp.einsum('ij,bje->bie', before, used, preferred_element_type=jnp.float32)
    blk_tot = jnp.sum(used.astype(jnp.float32), axis=1)
    blk_before = jnp.cumsum(blk_tot, axis=0) - blk_tot
    rank = (within + blk_before[:, None, :]).reshape(N, n_experts).astype(jnp.int32)
    cnt = jnp.sum(blk_tot, axis=0).astype(jnp.int32)
    padded = (cnt + ROW_BLOCK - 1) // ROW_BLOCK * ROW_BLOCK
    pend = jnp.cumsum(padded).astype(jnp.int32)
    row0 = rank + (pend - padded)[None, :]
    pos = jnp.sum(jnp.where(hit, row0[:, None, :], 0), axis=2)
    n_blk = (nk + n_experts * (ROW_BLOCK - 1) + ROW_BLOCK - 1) // ROW_BLOCK
    tok = jnp.broadcast_to(jnp.arange(N, dtype=jnp.int32)[:, None], (N, K))
    row_tok = jnp.zeros((n_blk * ROW_BLOCK,), jnp.int32).at[pos.reshape(nk)].set(tok.reshape(nk))
    blk_start = jnp.arange(n_blk, dtype=jnp.int32) * ROW_BLOCK
    blk_e = jnp.minimum(jnp.searchsorted(pend, blk_start, side='right'), n_experts - 1).astype(jnp.int32)
    n_valid_blk = (pend[-1] // ROW_BLOCK).astype(jnp.int32).reshape(1)
    return blk_e, n_valid_blk, row_tok, pos


def _expert_ffn_body(blk_e_ref, nvb_ref, tok_cur_ref, tok_nxt_ref, x_hbm, w13_ref, w2_ref, y_ref, xbuf, sem):
    i = pl.program_id(0)
    nvb = nvb_ref[0]
    slot = lax.rem(i, 2)
    de = w2_ref.shape[1]
    sub = w13_ref.shape[1] // x_hbm.shape[1]

    def row_copy(tok_ref, r, s):
        t = pl.multiple_of(tok_ref[0, 0, r] * sub, sub)
        return pltpu.make_async_copy(x_hbm.at[pl.ds(t, sub)], xbuf.at[s, pl.ds(r * sub, sub)], sem.at[s])

    def start_gather(tok_ref, s):
        for r in range(ROW_BLOCK):
            row_copy(tok_ref, r, s).start()

    @pl.when((i == 0) & (nvb > 0))
    def _():
        start_gather(tok_cur_ref, 0)

    @pl.when(i + 1 < nvb)
    def _():
        start_gather(tok_nxt_ref, 1 - slot)

    @pl.when(i < nvb)
    def _():
        pltpu.make_async_copy(x_hbm.at[pl.ds(0, ROW_BLOCK * sub)], xbuf.at[slot], sem.at[slot]).wait()
        xb = jnp.concatenate([xbuf[slot, pl.ds(j, ROW_BLOCK, stride=sub), :] for j in range(sub)],
                             axis=1).astype(jnp.bfloat16)
        h13 = jnp.dot(xb, w13_ref[0], preferred_element_type=jnp.float32)
        h = (jax.nn.silu(h13[:, :de]) * h13[:, de:]).astype(jnp.bfloat16)
        y = jnp.dot(h, w2_ref[0], preferred_element_type=jnp.float32)
        lanes = y_ref.shape[1]
        for j in range(sub):
            y_ref[pl.ds(j, ROW_BLOCK, stride=sub), :] = y[:, j * lanes:(j + 1) * lanes]

    @pl.when(i >= nvb)
    def _():
        y_ref[...] = jnp.zeros_like(y_ref)


def expert_ffn_rows(xs, blk_e, n_valid_blk, row_tok, w13, w2):
    D = w13.shape[1]
    lanes = xs.shape[1]
    n_blk = blk_e.shape[0]
    de2 = w13.shape[2]
    tok3 = row_tok.reshape(n_blk, 1, ROW_BLOCK)
    last = n_blk - 1
    grid_spec = pltpu.PrefetchScalarGridSpec(
        num_scalar_prefetch=2,
        grid=(n_blk,),
        in_specs=[
            pl.BlockSpec((1, 1, ROW_BLOCK), lambda i, be, nv: (i, 0, 0), memory_space=pltpu.SMEM),
            pl.BlockSpec((1, 1, ROW_BLOCK), lambda i, be, nv: (jnp.minimum(i + 1, last), 0, 0),
                         memory_space=pltpu.SMEM),
            pl.BlockSpec(memory_space=pl.ANY),
            pl.BlockSpec((1, D, de2), lambda i, be, nv: (be[i], 0, 0)),
            pl.BlockSpec((1, de2 // 2, D), lambda i, be, nv: (be[i], 0, 0)),
        ],
        out_specs=pl.BlockSpec((ROW_BLOCK * (D // lanes), lanes), lambda i, be, nv: (i, 0)),
        scratch_shapes=[pltpu.VMEM((2, ROW_BLOCK * (D // lanes), lanes), jnp.float32),
                        pltpu.SemaphoreType.DMA((2,))],
    )
    return pl.pallas_call(
        _expert_ffn_body,
        grid_spec=grid_spec,
        out_shape=jax.ShapeDtypeStruct((n_blk * ROW_BLOCK * (D // lanes), lanes), jnp.float32),
        compiler_params=pltpu.CompilerParams(dimension_semantics=("arbitrary",), vmem_limit_bytes=VMEM_LIMIT),
    )(blk_e, n_valid_blk, tok3, tok3, xs, w13, w2)


def _combine_body(pos_cur_ref, pos_nxt_ref, y_hbm, wts_ref, hin_ref, hres_ref, gt_ref, sw13_ref, sw2_ref,
                  fg_ref, out_ref, ybuf, sem):
    j = pl.program_id(0)
    nt = pl.num_programs(0)
    slot = lax.rem(j, 2)
    K = wts_ref.shape[1]
    tm = wts_ref.shape[0]
    ds = sw2_ref.shape[0]
    sub = hin_ref.shape[1] // y_hbm.shape[1]

    def row_copy(pos_ref, k, r, s):
        p = pl.multiple_of(pos_ref[0, 0, k * tm + r] * sub, sub)
        return pltpu.make_async_copy(y_hbm.at[pl.ds(p, sub)], ybuf.at[s, k, pl.ds(r * sub, sub)], sem.at[s])

    def start_gather(pos_ref, s):
        for k in range(K):
            for r in range(tm):
                row_copy(pos_ref, k, r, s).start()

    @pl.when(j == 0)
    def _():
        start_gather(pos_cur_ref, 0)

    @pl.when(j + 1 < nt)
    def _():
        start_gather(pos_nxt_ref, 1 - slot)

    for k in range(K):
        pltpu.make_async_copy(y_hbm.at[pl.ds(0, tm * sub)], ybuf.at[slot, k], sem.at[slot]).wait()

    def routed(k):
        rows = ybuf.at[slot, k]
        return jnp.concatenate([rows[pl.ds(i, tm, stride=sub), :] for i in range(sub)], axis=1)

    w = wts_ref[...]
    acc = routed(0) * w[:, 0:1]
    for k in range(1, K):
        acc = acc + routed(k) * w[:, k:k + 1]
    hb = hin_ref[...].astype(jnp.bfloat16)
    s13 = jnp.dot(hb, sw13_ref[...], preferred_element_type=jnp.float32)
    sh = (jax.nn.silu(s13[:, :ds]) * s13[:, ds:]).astype(jnp.bfloat16)
    acc = acc + jnp.dot(sh, sw2_ref[...], preferred_element_type=jnp.float32)
    h = hres_ref[...] + gt_ref[0] * acc
    ms = jnp.mean(h * h, axis=-1, keepdims=True)
    out_ref[...] = h * lax.rsqrt(ms + NORM_EPS) * fg_ref[...]


def moe_combine_norm(y_rows, pos, wts, h_in, h_res, gate, sw13, sw2, final_g, seq):
    N, D = h_in.shape
    K = pos.shape[1]
    lanes = y_rows.shape[1]
    tm = COMBINE_ROWS
    nt = N // tm
    tiles_per_seq = seq // tm
    pos3 = pos.reshape(nt, tm, K).transpose(0, 2, 1).reshape(nt, 1, K * tm)
    last = nt - 1
    return pl.pallas_call(
        _combine_body,
        grid=(nt,),
        in_specs=[
            pl.BlockSpec((1, 1, K * tm), lambda j: (j, 0, 0), memory_space=pltpu.SMEM),
            pl.BlockSpec((1, 1, K * tm), lambda j: (jnp.minimum(j + 1, last), 0, 0), memory_space=pltpu.SMEM),
            pl.BlockSpec(memory_space=pl.ANY),
            pl.BlockSpec((tm, K), lambda j: (j, 0)),
            pl.BlockSpec((tm, D), lambda j: (j, 0)),
            pl.BlockSpec((tm, D), lambda j: (j, 0)),
            pl.BlockSpec((1, 1, D), lambda j: (j // tiles_per_seq, 0, 0)),
            pl.BlockSpec(sw13.shape, lambda j: (0, 0)),
            pl.BlockSpec(sw2.shape, lambda j: (0, 0)),
            pl.BlockSpec((1, D), lambda j: (0, 0)),
        ],
        out_specs=pl.BlockSpec((tm, D), lambda j: (j, 0)),
        out_shape=jax.ShapeDtypeStruct((N, D), jnp.float32),
        scratch_shapes=[pltpu.VMEM((2, K, tm * (D // lanes), lanes), jnp.float32), pltpu.SemaphoreType.DMA((2,))],
        compiler_params=pltpu.CompilerParams(dimension_semantics=("arbitrary",), vmem_limit_bytes=VMEM_LIMIT),
    )(pos3, pos3, y_rows, wts, h_in, h_res, gate.reshape(-1, 1, D), sw13, sw2, final_g.reshape(1, D))


def moe_block_final(h_in, h_slab, h_res, gate, top_e, wts, w1, w3, w2, sw1, sw3, sw2, final_g, seq):
    bf = jnp.bfloat16
    blk_e, n_valid_blk, row_tok, pos = moe_layout(top_e, w1.shape[0])
    w13 = jnp.concatenate([w1, w3], axis=-1).astype(bf)
    y_rows = expert_ffn_rows(h_slab, blk_e, n_valid_blk, row_tok, w13, w2.astype(bf))
    sw13 = jnp.concatenate([sw1, sw3], axis=-1).astype(bf)
    return moe_combine_norm(y_rows, pos, wts, h_in, h_res, gate, sw13, sw2.astype(bf), final_g, seq)


def kernel(x, c, ctx, c_ctx, w_ada, b_ada, norm1_g, w_in, mu_shift, w0, w_up, a0, a_up, g_up,
           k_k, k_a, r_k, lnx_w, lnx_b, conv_w, A_log, dt_bias, onorm_g, w_out, norm2_g,
           router_w, router_b, exp_w1, exp_w3, exp_w2, sh_w1, sh_w3, sh_w2, final_g):
    h_x, h_c = x, ctx
    l = 0
    mod = jax.nn.silu(c) @ w_ada[l] + b_ada[l]
    mod_c = jax.nn.silu(c_ctx) @ w_ada[l] + b_ada[l]
    sh1, sc1, gt1, sh2, sc2, gt2 = (m[:, None] for m in jnp.split(mod, 6, axis=-1))
    sh1c, sc1c, gt1c, sh2c, sc2c, gt2c = jnp.split(mod_c, 6, axis=-1)

    hm_x = rmsnorm(h_x, norm1_g[l]) * (1.0 + sc1) + sh1
    hm_c = rmsnorm(h_c, norm1_g[l]) * (1.0 + sc1c) + sh1c
    w_in_a, w_in_b = w_in[l][:, :RWKV_COLS], w_in[l][:, RWKV_COLS:]
    ya_x = rwkv7_mixer_latent(hm_c @ w_in_a, hm_x @ w_in_a, mu_shift[l], w0[l], w_up[l],
                              a0[l], a_up[l], g_up[l], k_k[l], k_a[l], r_k[l], lnx_w[l], lnx_b[l])
    yb_x = gdn_mixer_latent(hm_c @ w_in_b, hm_x @ w_in_b, conv_w[l], A_log[l], dt_bias[l], onorm_g[l])
    B, T, D = h_x.shape
    N = B * T
    h_res, h_in, h_slab, top_e, wts = mixer_out_route(
        ya_x.reshape(N, DA), yb_x.reshape(N, DB), x.reshape(N, D), gt1[:, 0], sc2[:, 0], sh2[:, 0],
        w_out[l], norm2_g[l], router_w[l], router_b[l], T)
    out = moe_block_final(h_in, h_slab, h_res, gt2[:, 0], top_e.T, wts.T,
                          exp_w1[l], exp_w3[l], exp_w2[l], sh_w1[l], sh_w3[l], sh_w2[l], final_g, T)
    return out.reshape(B, T, D)
```

```python
import functools

import jax
import jax.numpy as jnp
from jax import lax
from jax.experimental import pallas as pl
from jax.experimental.pallas import tpu as pltpu

D_MODEL = 1024
BATCH = 32
SEQ = 2048
DEPTH = 1
CTX_LEN = 256
GRID_W = 64
D_MIX = D_MODEL
DA = D_MIX // 2
NA = 64
HA = DA // NA
DECAY_LORA = 64
AAA_LORA = 64
GATE_LORA = 128
RWKV_COLS = 3 * DA + 2 * DECAY_LORA + 2 * AAA_LORA + GATE_LORA
RWKV_EPS = 64e-5
DB = D_MIX - DA
HB = 4
KB = DB // HB
GDN_CONV = 5
GDN_CHUNK = 64
GDN_COLS = 4 * DB + 4 * HB
IN_COLS = RWKV_COLS + GDN_COLS
N_EXPERTS = 256
TOP_K = 8
N_GROUPS = 8
TOPK_GROUPS = 4
D_EXPERT = 256
D_SHARED = 256
ROUTED_SCALE = 2.5
MOE_BLOCK = 256
NORM_EPS = 1e-6


def rmsnorm(x, g):
    xf = x.astype(jnp.float32)
    y = xf * lax.rsqrt(jnp.mean(xf * xf, axis=-1, keepdims=True) + NORM_EPS)
    return (y * g.astype(jnp.float32)).astype(x.dtype)


def l2norm(x):
    xf = x.astype(jnp.float32)
    return xf * lax.rsqrt(jnp.sum(xf * xf, axis=-1, keepdims=True) + 1e-6)


def shift_seq(p):
    h = p.shape[-1] // 2
    prev = jnp.pad(p[:, :-1, :h], ((0, 0), (1, 0), (0, 0)))
    nxt = jnp.pad(p[:, 1:, h:], ((0, 0), (0, 1), (0, 0)))
    return jnp.concatenate([prev, nxt], axis=-1)


def shift_grid(p):
    B, T, C = p.shape
    rows = T // GRID_W
    q = C // 4
    g = p.reshape(B, rows, GRID_W, C)
    left = jnp.pad(g[:, :, :-1, :q], ((0, 0), (0, 0), (1, 0), (0, 0)))
    right = jnp.pad(g[:, :, 1:, q:2 * q], ((0, 0), (0, 0), (0, 1), (0, 0)))
    up = jnp.pad(g[:, :-1, :, 2 * q:3 * q], ((0, 0), (1, 0), (0, 0), (0, 0)))
    down = jnp.pad(g[:, 1:, :, 3 * q:], ((0, 0), (0, 1), (0, 0), (0, 0)))
    return jnp.concatenate([left, right, up, down], axis=-1).reshape(B, T, C)


def dwconv_centered(x, w):
    C = x.shape[-1]
    return lax.conv_general_dilated(
        x, w[:, None, :].astype(x.dtype), window_strides=(1,),
        padding=[(GDN_CONV // 2, GDN_CONV // 2)],
        dimension_numbers=('NWC', 'WIO', 'NWC'), feature_group_count=C)


def gdn_prepare(p, conv_w, A_log, dt_bias):
    B, T, _ = p.shape
    qkv = jax.nn.silu(dwconv_centered(p[..., :3 * DB], conv_w)).astype(jnp.float32)
    q, k, v = jnp.split(qkv, 3, axis=-1)
    q = l2norm(q.reshape(B, T, HB, KB)) * (KB ** -0.5)
    k = l2norm(k.reshape(B, T, HB, KB))
    v = v.reshape(B, T, HB, KB)
    z = p[..., 3 * DB:4 * DB]
    gl = p[..., 4 * DB:].astype(jnp.float32).reshape(B, T, 2, 2, HB)
    g = -jnp.exp(A_log) * jax.nn.softplus(gl[:, :, 0] + dt_bias)
    beta = jax.nn.sigmoid(gl[:, :, 1])
    return q, k, v, g, beta, z


WKV_CHUNK = 64
WKV_GROUP = 4
WKV_WIDTH = WKV_GROUP * NA
WKV_ROWS = 2


def _wkv_chunk_body(r_ref, v_ref, kk_ref, lw_ref, k_ref, b_ref, y_ref, st_ref):
    W = WKV_WIDTH
    fwd = pl.program_id(0) == 0

    @pl.when(pl.program_id(2) == 0)
    def _():
        st_ref[...] = jnp.zeros_like(st_ref)

    rows, n_groups = r_ref.shape[0], r_ref.shape[2] // W
    where = [(i, slice(g * W, (g + 1) * W)) for i in range(rows) for g in range(n_groups)]
    chains = [_wkv_chunk_math(fwd, r_ref[i, :, lanes], v_ref[i, :, lanes], kk_ref[i, :, lanes],
                              lw_ref[0, i, :, lanes], k_ref[0, i, :, lanes], b_ref[0, i, :, lanes], st_ref[s])
              for s, (i, lanes) in enumerate(where)]
    for s, (y, st) in enumerate(_interleave(chains)):
        i, lanes = where[s]
        y_ref[0, i, :, lanes] = y
        st_ref[s] = st


def _interleave(chains):
    results = [None] * len(chains)
    live = list(range(len(chains)))
    while live:
        for i in list(live):
            try:
                next(chains[i])
            except StopIteration as stop:
                results[i] = stop.value
                live.remove(i)
    return results


def _wkv_chunk_math(fwd, r, v, kk, lw, k, b, st):
    C, W, G = WKV_CHUNK, WKV_WIDTH, WKV_GROUP
    f32, bf16 = jnp.float32, jnp.bfloat16
    sgn = jnp.where(fwd, 1, -1)

    row = lax.broadcasted_iota(jnp.int32, (C, C), 0)
    col = lax.broadcasted_iota(jnp.int32, (C, C), 1)
    tri = jnp.where((col - row) * sgn <= 0, 1.0, 0.0).astype(bf16)
    hi = lw.astype(bf16)
    rem = lw - hi.astype(f32)
    mid = rem.astype(bf16)
    lo = (rem - mid.astype(f32)).astype(bf16)
    dot = functools.partial(jnp.dot, preferred_element_type=f32)
    cinc = dot(tri, hi) + dot(tri, mid) + dot(tri, lo)
    yield
    ctot = jnp.where(fwd, cinc[C - 1:C], cinc[0:1])
    e_out = jnp.exp(-cinc)
    e_last = jnp.exp(ctot - cinc)

    rr = lax.broadcasted_iota(jnp.int32, (G * C, W), 0)
    cc = lax.broadcasted_iota(jnp.int32, (G * C, W), 1)
    same_head = (rr // C) == (cc // NA)

    def bdiag(x):
        return jnp.where(same_head, jnp.concatenate([x] * G, axis=0), 0.0).astype(bf16)

    lhs = jnp.concatenate([bdiag(kk * jnp.exp(cinc - lw)), bdiag(r * jnp.exp(cinc))], axis=0)
    rhs = jnp.concatenate([bdiag(b * e_out), bdiag(k * e_out)], axis=0)
    trans_b = (((1,), (1,)), ((), ()))
    gram = lax.dot_general(lhs, rhs, trans_b, preferred_element_type=f32)
    hs = lax.dot_general(lhs, st.astype(bf16), trans_b, preferred_element_type=f32)
    yield

    tr, tc = rr % C, cc % C
    order = (tc - tr) * sgn
    strict = order < 0
    incl = order <= 0
    n = G * C
    a_bk = jnp.where(strict, gram[:n, :n], 0.0)
    a_kk = jnp.where(strict, gram[:n, n:], 0.0)
    rb = jnp.where(incl, gram[n:, :n], 0.0)
    rk = jnp.where(incl, gram[n:, n:], 0.0)

    vbd = bdiag(v)
    x = hs[:n] + dot(a_kk.astype(bf16), vbd)
    p = -a_bk
    steps = C.bit_length() - 1
    for i in range(steps):
        yield
        pb = p.astype(bf16)
        x = x + dot(pb, x.astype(bf16))
        if i + 1 < steps:
            p = dot(pb, pb)
    ub = x.astype(bf16)
    yield

    y = hs[n:] + dot(jnp.concatenate([rk, -rb], axis=1).astype(bf16), jnp.concatenate([vbd, ub], axis=0))
    yield

    trans_a = (((0,), (0,)), ((), ()))
    upd = lax.dot_general(jnp.concatenate([vbd, -ub], axis=0),
                          jnp.concatenate([bdiag(k * e_last), bdiag(b * e_last)], axis=0),
                          trans_a, preferred_element_type=f32)
    return sum(y[h * C:(h + 1) * C] for h in range(G)), st * jnp.exp(ctot) + upd


def wkv7_chunked(r, v, kk, lw, k_dir, b_dir, n_ctx_chunks):
    B, T, da = r.shape
    C, W = WKV_CHUNK, WKV_WIDTH
    n = T // C

    def chunk(d, c):
        back = jnp.where(c < n_ctx_chunks, n_ctx_chunks - 1 - c, n + n_ctx_chunks - 1 - c)
        return jnp.where(d == 0, c, back)

    rows = WKV_ROWS
    shared = pl.BlockSpec((rows, C, da), lambda d, bi, c: (bi, chunk(d, c), 0))
    per_dir = pl.BlockSpec((1, rows, C, da), lambda d, bi, c: (d, bi, chunk(d, c), 0))
    return pl.pallas_call(
        _wkv_chunk_body,
        grid=(2, B // rows, n),
        in_specs=[shared, shared, shared, per_dir, per_dir, per_dir],
        out_specs=per_dir,
        out_shape=jax.ShapeDtypeStruct((2, B, T, da), jnp.float32),
        scratch_shapes=[pltpu.VMEM((rows * (da // W), W, W), jnp.float32)],
        compiler_params=pltpu.CompilerParams(dimension_semantics=("arbitrary",) * 3,
                                             vmem_limit_bytes=VMEM_LIMIT),
    )(r, v, kk, lw, k_dir, b_dir)


def rwkv7_prepare_bm(p, shifted, mu, w0, w_up, a0, a_up, g_up, k_k, k_a):
    p = (p + mu * (shifted - p)).astype(jnp.float32)
    B, T = p.shape[:2]
    cuts = [DA, 2 * DA, 3 * DA, 3 * DA + 2 * DECAY_LORA, 3 * DA + 2 * DECAY_LORA + 2 * AAA_LORA]
    r, k, v, wl, al, gl = jnp.split(p, cuts, axis=-1)
    wl = wl.reshape(B, T, 2, DECAY_LORA)
    al = al.reshape(B, T, 2, AAA_LORA)
    w_log = -jax.nn.softplus(-(w0[:, None, None] + jnp.einsum('btdl,dlc->dbtc', jnp.tanh(wl), w_up))) - 0.5
    lw = -jnp.exp(w_log)
    a = jax.nn.sigmoid(a0[:, None, None] + jnp.einsum('btdl,dlc->dbtc', al, a_up))
    g = jax.nn.sigmoid(gl) @ g_up
    kk = l2norm((k * k_k).reshape(B, T, HA, NA)).reshape(B, T, DA)
    k_dir = k[None] * (1.0 + (a - 1.0) * k_a)
    b_dir = kk[None] * a
    return r, v, kk, g, lw, k_dir, b_dir


def rwkv7_mixer_latent(p_ctx, p_lat, mu, w0, w_up, a0, a_up, g_up, k_k, k_a, r_k, lnx_w, lnx_b):
    t_ctx = p_ctx.shape[1]
    p = jnp.concatenate([p_ctx, p_lat], axis=1)
    shifted = jnp.concatenate([shift_seq(p_ctx), shift_grid(p_lat)], axis=1)
    r, v, kk, g, lw, k_dir, b_dir = rwkv7_prepare_bm(p, shifted, mu, w0, w_up, a0, a_up, g_up, k_k, k_a)
    y2 = wkv7_chunked(r, v, kk, lw, k_dir, b_dir, t_ctx // WKV_CHUNK)
    sl = lambda t: t[..., t_ctx:, :]
    r, v, g, k_dir = sl(r), sl(v), sl(g), sl(k_dir)
    B, T = r.shape[:2]
    y = (sl(y2[0]) + sl(y2[1])).reshape(B, T, HA, NA)
    m = jnp.mean(y, axis=-1, keepdims=True)
    var = jnp.mean(jnp.square(y - m), axis=-1, keepdims=True)
    y = ((y - m) * lax.rsqrt(var + RWKV_EPS)).reshape(B, T, DA) * lnx_w + lnx_b
    heads = lambda t: t.reshape(t.shape[:-1] + (HA, NA))
    bonus = jnp.sum(heads(r)[None] * heads(k_dir) * r_k, axis=(0, 4))
    return (y + (bonus[..., None] * heads(v)).reshape(B, T, DA)) * g


GDN_ROWS = 4


def _gdn_chunk_body(q_ref, k_ref, v_ref, gc_ref, grow_ref, beta_ref, o_ref, st_ref):
    fwd = pl.program_id(0) == 0

    @pl.when(pl.program_id(2) == 0)
    def _():
        st_ref[...] = jnp.zeros_like(st_ref)

    def over_lanes(a):
        return jnp.concatenate([jnp.broadcast_to(a[:, h:h + 1], (a.shape[0], KB)) for h in range(HB)], axis=1)

    chains = [_gdn_chunk_math(fwd, q_ref[i], k_ref[i], v_ref[i], over_lanes(gc_ref[0, i]), grow_ref[0, i, 0],
                              over_lanes(beta_ref[0, i]), st_ref[i]) for i in range(q_ref.shape[0])]
    for i, (o, st) in enumerate(_interleave(chains)):
        o_ref[0, i] = o
        st_ref[i] = st


def _gdn_chunk_math(fwd, q, k, v, gcx, grow, bx, st):
    C, H, KD = GDN_CHUNK, HB, KB
    W, n = H * KD, H * C
    f32, bf16 = jnp.float32, jnp.bfloat16
    sgn = jnp.where(fwd, 1, -1)
    dot = functools.partial(jnp.dot, preferred_element_type=f32)
    head = lambda x, h: x[:, h * KD:(h + 1) * KD]

    rr = lax.broadcasted_iota(jnp.int32, (n, W), 0)
    cc = lax.broadcasted_iota(jnp.int32, (n, W), 1)
    same_head = (rr // C) == (cc // KD)

    def bdiag(x):
        return jnp.where(same_head, jnp.concatenate([x] * H, axis=0), 0.0).astype(bf16)

    eg = jnp.exp(gcx)
    kb = k * bx
    trans_b = (((1,), (1,)), ((), ()))
    gram = lax.dot_general(jnp.concatenate([bdiag(kb), bdiag(q)], axis=0), bdiag(k), trans_b,
                           preferred_element_type=f32)
    yield

    r2 = lax.broadcasted_iota(jnp.int32, (n, n), 0)
    c2 = lax.broadcasted_iota(jnp.int32, (n, n), 1)
    order = (c2 % C - r2 % C) * sgn
    same2 = (r2 // C) == (c2 // C)
    incl = same2 & (order <= 0)
    strict = same2 & (order < 0)
    g_rows = jnp.concatenate([jnp.concatenate([head(gcx, h)] * (n // KD), axis=1) for h in range(H)], axis=0)
    decay = jnp.exp(jnp.where(incl, g_rows - grow, 0.0))
    a_low = jnp.where(strict, gram[:n] * decay, 0.0)
    attn = jnp.where(incl, gram[n:] * decay, 0.0)

    vb, kbe = v * bx, kb * eg
    x = jnp.concatenate([jnp.concatenate([head(vb, h), head(kbe, h)], axis=1) for h in range(H)], axis=0)
    p = -a_low
    steps = C.bit_length() - 1
    for i in range(steps):
        yield
        pb = p.astype(bf16)
        x = x + dot(pb, x.astype(bf16))
        if i + 1 < steps:
            p = dot(pb, pb)
    u, w = x[:, :KD], x[:, KD:]
    yield

    stb = st.astype(bf16)
    w_bd = jnp.where(same_head, jnp.concatenate([w] * H, axis=1), 0.0).astype(bf16)
    v_new = (u - dot(w_bd, stb)).astype(bf16)
    yield
    o = dot(jnp.concatenate([bdiag(q * eg), attn.astype(bf16)], axis=1), jnp.concatenate([stb, v_new], axis=0))
    yield

    g_last = jnp.where(fwd, gcx[C - 1:C], gcx[0:1])
    trans_a = (((0,), (0,)), ((), ()))
    upd = lax.dot_general(bdiag(k * jnp.exp(g_last - gcx)), v_new, trans_a, preferred_element_type=f32)
    e_last = jnp.exp(g_last)
    st_new = jnp.concatenate([st[h * KD:(h + 1) * KD] * head(e_last, h) for h in range(H)], axis=0) + upd
    return jnp.concatenate([o[h * C:(h + 1) * C] for h in range(H)], axis=1), st_new


def gdn_chunked_pallas(q, k, v, gc, grow, beta, n_ctx_chunks):
    B, T, W = q.shape
    C = GDN_CHUNK
    n = T // C

    def chunk(d, c):
        back = jnp.where(c < n_ctx_chunks, n_ctx_chunks - 1 - c, n + n_ctx_chunks - 1 - c)
        return jnp.where(d == 0, c, back)

    nb = GDN_ROWS
    shared = pl.BlockSpec((nb, C, W), lambda d, bi, c: (bi, chunk(d, c), 0))
    per_dir = pl.BlockSpec((1, nb, C, W), lambda d, bi, c: (d, bi, chunk(d, c), 0))
    rows = pl.BlockSpec((1, nb, 1, 1, HB * C), lambda d, bi, c: (d, bi, chunk(d, c), 0, 0))
    per_head = pl.BlockSpec((1, nb, C, HB), lambda d, bi, c: (d, bi, chunk(d, c), 0))
    return pl.pallas_call(
        _gdn_chunk_body,
        grid=(2, B // nb, n),
        in_specs=[shared, shared, shared, per_head, rows, per_head],
        out_specs=per_dir,
        out_shape=jax.ShapeDtypeStruct((2, B, T, W), jnp.float32),
        scratch_shapes=[pltpu.VMEM((nb, W, KB), jnp.float32)],
        compiler_params=pltpu.CompilerParams(dimension_semantics=("arbitrary",) * 3,
                                             vmem_limit_bytes=VMEM_LIMIT),
    )(q, k, v, gc, grow, beta)


def gdn_mixer_latent(p_ctx, p_lat, conv_w, A_log, dt_bias, onorm_g):
    t_ctx = p_ctx.shape[1]
    qc, kc, vc, gc_, bc, _ = gdn_prepare(p_ctx, conv_w, A_log, dt_bias)
    ql, kl, vl, gl_, bl, z = gdn_prepare(p_lat, conv_w, A_log, dt_bias)
    cat = lambda a, b: jnp.concatenate([a, b], axis=1)
    B, T = ql.shape[0], t_ctx + ql.shape[1]
    C = GDN_CHUNK
    flat = lambda t: t.reshape(B, T, DB)
    q, k, v = flat(cat(qc, ql)), flat(cat(kc, kl)), flat(cat(vc, vl))
    g = jnp.moveaxis(cat(gc_, gl_), 2, 0).reshape(2, B, T // C, C, HB)
    gcs = jnp.stack([jnp.cumsum(g[0], axis=2), lax.cumsum(g[1], axis=2, reverse=True)])
    grow = jnp.swapaxes(gcs, 3, 4).reshape(2, B, T // C, 1, HB * C)
    beta = jnp.moveaxis(cat(bc, bl), 2, 0)
    o2 = gdn_chunked_pallas(q, k, v, gcs.reshape(2, B, T, HB), grow, beta, t_ctx // C)
    o = (o2[0, :, t_ctx:] + o2[1, :, t_ctx:]).reshape(B, T - t_ctx, HB, KB)
    o = o * lax.rsqrt(jnp.mean(o * o, axis=-1, keepdims=True) + NORM_EPS) * onorm_g
    return o.reshape(B, T - t_ctx, DB) * jax.nn.silu(z.astype(jnp.float32))


LANES = 128
ROW_BLOCK = 256
COMBINE_ROWS = 128
LAYOUT_TOKENS = 256
VMEM_LIMIT = 48 * 1024 * 1024


def moe_route(xs, router_w, router_b):
    N = xs.shape[0]
    E = router_w.shape[1]
    scores = jax.nn.sigmoid((xs @ router_w).astype(jnp.float32))
    sel = scores + router_b.astype(jnp.float32)
    grp_score = jnp.sum(lax.top_k(sel.reshape(N, N_GROUPS, E // N_GROUPS), 2)[0], axis=-1)
    top_g = lax.top_k(grp_score, TOPK_GROUPS)[1]
    gmask = jnp.any(top_g[:, :, None] == jnp.arange(N_GROUPS)[None, None, :], axis=1)
    emask = jnp.repeat(gmask, E // N_GROUPS, axis=1)
    top_e = lax.top_k(jnp.where(emask, sel, -jnp.inf), TOP_K)[1]
    wts = jnp.take_along_axis(scores, top_e, axis=1)
    wts = wts / jnp.sum(wts, axis=-1, keepdims=True) * ROUTED_SCALE
    return top_e, wts


MID_ROWS = 256


def _first_max(x, idx, axis_len):
    m = jnp.max(x, axis=0, keepdims=True)
    i = jnp.min(jnp.where(x == m, idx, axis_len), axis=0, keepdims=True)
    return m, i


def _mid_body(ya_ref, yb_ref, x_ref, gt1_ref, sc2_ref, sh2_ref, wo_ref, g2_ref, rwt_ref, rb_ref,
              hx_ref, hin_ref, slab_ref, te_ref, wt_ref):
    f32, bf16 = jnp.float32, jnp.bfloat16
    half = ya_ref.shape[1]
    mix = (jnp.dot(ya_ref[...].astype(bf16), wo_ref[:half], preferred_element_type=f32)
           + jnp.dot(yb_ref[...].astype(bf16), wo_ref[half:], preferred_element_type=f32))
    hx = x_ref[...] + gt1_ref[0] * mix
    hx_ref[...] = hx
    ms = jnp.mean(hx * hx, axis=-1, keepdims=True)
    hin = hx * lax.rsqrt(ms + NORM_EPS) * g2_ref[...] * (1.0 + sc2_ref[0]) + sh2_ref[0]
    hin_ref[...] = hin
    lanes = slab_ref.shape[1]
    sub = hin.shape[1] // lanes
    for j in range(sub):
        slab_ref[pl.ds(j, hin.shape[0], stride=sub), :] = hin[:, j * lanes:(j + 1) * lanes]

    logits = lax.dot_general(rwt_ref[...], hin.astype(bf16), (((1,), (1,)), ((), ())),
                             preferred_element_type=f32)
    n_e, tm = logits.shape
    per_group = n_e // N_GROUPS
    scores = jax.nn.sigmoid(logits)
    sel = scores + rb_ref[...]
    neg = -jnp.inf

    lidx = lax.broadcasted_iota(jnp.int32, (per_group, tm), 0)
    gs = []
    for g in range(N_GROUPS):
        blk = sel[g * per_group:(g + 1) * per_group]
        m1, i1 = _first_max(blk, lidx, per_group)
        m2 = jnp.max(jnp.where(lidx == i1, neg, blk), axis=0, keepdims=True)
        gs.append(m1 + m2)
    gsc = jnp.concatenate(gs, axis=0)

    gidx = lax.broadcasted_iota(jnp.int32, (N_GROUPS, tm), 0)
    chosen = jnp.zeros((N_GROUPS, tm), f32)
    for _ in range(TOPK_GROUPS):
        _, gi = _first_max(gsc, gidx, N_GROUPS)
        hit = gidx == gi
        chosen = jnp.where(hit, 1.0, chosen)
        gsc = jnp.where(hit, neg, gsc)
    allowed = jnp.concatenate([jnp.broadcast_to(chosen[g:g + 1], (per_group, tm)) for g in range(N_GROUPS)], axis=0)
    masked = jnp.where(allowed > 0.5, sel, neg)

    eidx = lax.broadcasted_iota(jnp.int32, (n_e, tm), 0)
    ids, ws = [], []
    for _ in range(TOP_K):
        _, ei = _first_max(masked, eidx, n_e)
        hit = eidx == ei
        ids.append(ei)
        ws.append(jnp.sum(jnp.where(hit, scores, 0.0), axis=0, keepdims=True))
        masked = jnp.where(hit, neg, masked)
    w = jnp.concatenate(ws, axis=0)
    te_ref[...] = jnp.concatenate(ids, axis=0)
    wt_ref[...] = w / jnp.sum(w, axis=0, keepdims=True) * ROUTED_SCALE


def mixer_out_route(ya, yb, x, gt1, sc2, sh2, w_out, norm2_g, router_w, router_b, seq):
    N, D = x.shape
    sub = D // LANES
    E = router_w.shape[1]
    tm = MID_ROWS
    per_seq = seq // tm
    row = lambda i: (i, 0)
    per_batch = pl.BlockSpec((1, 1, D), lambda i: (i // per_seq, 0, 0))
    whole = lambda a: pl.BlockSpec(a.shape, lambda i: (0,) * a.ndim)
    wo = w_out.astype(jnp.bfloat16)
    rwt = router_w.T.astype(jnp.bfloat16)
    g2 = norm2_g.reshape(1, D)
    rb = router_b.reshape(E, 1).astype(jnp.float32)
    b3 = lambda a: a.reshape(-1, 1, D)
    return pl.pallas_call(
        _mid_body,
        grid=(N // tm,),
        in_specs=[pl.BlockSpec((tm, D // 2), row), pl.BlockSpec((tm, D // 2), row), pl.BlockSpec((tm, D), row),
                  per_batch, per_batch, per_batch, whole(wo), whole(g2), whole(rwt), whole(rb)],
        out_specs=[pl.BlockSpec((tm, D), row), pl.BlockSpec((tm, D), row), pl.BlockSpec((tm * sub, LANES), row),
                   pl.BlockSpec((TOP_K, tm), lambda i: (0, i)), pl.BlockSpec((TOP_K, tm), lambda i: (0, i))],
        out_shape=[jax.ShapeDtypeStruct((N, D), jnp.float32), jax.ShapeDtypeStruct((N, D), jnp.float32),
                   jax.ShapeDtypeStruct((N * sub, LANES), jnp.float32),
                   jax.ShapeDtypeStruct((TOP_K, N), jnp.int32), jax.ShapeDtypeStruct((TOP_K, N), jnp.float32)],
        compiler_params=pltpu.CompilerParams(dimension_semantics=("arbitrary",), vmem_limit_bytes=VMEM_LIMIT),
    )(ya, yb, x, b3(gt1), b3(sc2), b3(sh2), wo, g2, rwt, rb)


def moe_layout(top_e, n_experts):
    N, K = top_e.shape
    nk = N * K
    tb = LAYOUT_TOKENS
    eids = jnp.arange(n_experts, dtype=jnp.int32)
    hit = top_e[:, :, None] == eids[None, None, :]
    used = jnp.any(hit, axis=1).astype(jnp.bfloat16).reshape(N // tb, tb, n_experts)
    before = (jnp.arange(tb)[:, None] > jnp.arange(tb)[None, :]).astype(jnp.bfloat16)
    within = jnp.einsum('ij,bje->bie', before, used, preferred_element_type=jnp.float32)
    blk_tot = jnp.sum(used.astype(jnp.float32), axis=1)
    blk_before = jnp.cumsum(blk_tot, axis=0) - blk_tot
    rank = (within + blk_before[:, None, :]).reshape(N, n_experts).astype(jnp.int32)
    cnt = jnp.sum(blk_tot, axis=0).astype(jnp.int32)
    padded = (cnt + ROW_BLOCK - 1) // ROW_BLOCK * ROW_BLOCK
    pend = jnp.cumsum(padded).astype(jnp.int32)
    row0 = rank + (pend - padded)[None, :]
    pos = jnp.sum(jnp.where(hit, row0[:, None, :], 0), axis=2)
    n_blk = (nk + n_experts * (ROW_BLOCK - 1) + ROW_BLOCK - 1) // ROW_BLOCK
    tok = jnp.broadcast_to(jnp.arange(N, dtype=jnp.int32)[:, None], (N, K))
    row_tok = jnp.zeros((n_blk * ROW_BLOCK,), jnp.int32).at[pos.reshape(nk)].set(tok.reshape(nk))
    blk_start = jnp.arange(n_blk, dtype=jnp.int32) * ROW_BLOCK
    blk_e = jnp.minimum(jnp.searchsorted(pend, blk_start, side='right'), n_experts - 1).astype(jnp.int32)
    n_valid_blk = (pend[-1] // ROW_BLOCK).astype(jnp.int32).reshape(1)
    return blk_e, n_valid_blk, row_tok, pos


def _expert_ffn_body(blk_e_ref, nvb_ref, tok_cur_ref, tok_nxt_ref, x_hbm, w13_ref, w2_ref, y_ref, xbuf, sem):
    i = pl.program_id(0)
    nvb = nvb_ref[0]
    slot = lax.rem(i, 2)
    de = w2_ref.shape[1]
    sub = w13_ref.shape[1] // x_hbm.shape[1]

    def row_copy(tok_ref, r, s):
        t = pl.multiple_of(tok_ref[0, 0, r] * sub, sub)
        return pltpu.make_async_copy(x_hbm.at[pl.ds(t, sub)], xbuf.at[s, pl.ds(r * sub, sub)], sem.at[s])

    def start_gather(tok_ref, s):
        for r in range(ROW_BLOCK):
            row_copy(tok_ref, r, s).start(priority=r % 2)

    @pl.when((i == 0) & (nvb > 0))
    def _():
        start_gather(tok_cur_ref, 0)

    @pl.when(i + 1 < nvb)
    def _():
        start_gather(tok_nxt_ref, 1 - slot)

    @pl.when(i < nvb)
    def _():
        pltpu.make_async_copy(x_hbm.at[pl.ds(0, ROW_BLOCK * sub)], xbuf.at[slot], sem.at[slot]).wait()
        xb = jnp.concatenate([xbuf[slot, pl.ds(j, ROW_BLOCK, stride=sub), :] for j in range(sub)],
                             axis=1).astype(jnp.bfloat16)
        h13 = jnp.dot(xb, w13_ref[0], preferred_element_type=jnp.float32)
        h = (jax.nn.silu(h13[:, :de]) * h13[:, de:]).astype(jnp.bfloat16)
        y = jnp.dot(h, w2_ref[0], preferred_element_type=jnp.float32)
        lanes = y_ref.shape[1]
        for j in range(sub):
            y_ref[pl.ds(j, ROW_BLOCK, stride=sub), :] = y[:, j * lanes:(j + 1) * lanes]

    @pl.when(i >= nvb)
    def _():
        y_ref[...] = jnp.zeros_like(y_ref)


def expert_ffn_rows(xs, blk_e, n_valid_blk, row_tok, w13, w2):
    D = w13.shape[1]
    lanes = xs.shape[1]
    n_blk = blk_e.shape[0]
    de2 = w13.shape[2]
    tok3 = row_tok.reshape(n_blk, 1, ROW_BLOCK)
    last = n_blk - 1
    grid_spec = pltpu.PrefetchScalarGridSpec(
        num_scalar_prefetch=2,
        grid=(n_blk,),
        in_specs=[
            pl.BlockSpec((1, 1, ROW_BLOCK), lambda i, be, nv: (i, 0, 0), memory_space=pltpu.SMEM),
            pl.BlockSpec((1, 1, ROW_BLOCK), lambda i, be, nv: (jnp.minimum(i + 1, last), 0, 0),
                         memory_space=pltpu.SMEM),
            pl.BlockSpec(memory_space=pl.ANY),
            pl.BlockSpec((1, D, de2), lambda i, be, nv: (be[i], 0, 0)),
            pl.BlockSpec((1, de2 // 2, D), lambda i, be, nv: (be[i], 0, 0)),
        ],
        out_specs=pl.BlockSpec((ROW_BLOCK * (D // lanes), lanes), lambda i, be, nv: (i, 0)),
        scratch_shapes=[pltpu.VMEM((2, ROW_BLOCK * (D // lanes), lanes), jnp.float32),
                        pltpu.SemaphoreType.DMA((2,))],
    )
    return pl.pallas_call(
        _expert_ffn_body,
        grid_spec=grid_spec,
        out_shape=jax.ShapeDtypeStruct((n_blk * ROW_BLOCK * (D // lanes), lanes), jnp.float32),
        compiler_params=pltpu.CompilerParams(dimension_semantics=("arbitrary",), vmem_limit_bytes=VMEM_LIMIT),
    )(blk_e, n_valid_blk, tok3, tok3, xs, w13, w2)


def _combine_body(pos_cur_ref, pos_nxt_ref, y_hbm, wts_ref, hin_ref, hres_ref, gt_ref, sw13_ref, sw2_ref,
                  fg_ref, out_ref, ybuf, sem):
    j = pl.program_id(0)
    nt = pl.num_programs(0)
    slot = lax.rem(j, 2)
    K = wts_ref.shape[1]
    tm = wts_ref.shape[0]
    ds = sw2_ref.shape[0]
    sub = hin_ref.shape[1] // y_hbm.shape[1]

    def row_copy(pos_ref, k, r, s):
        p = pl.multiple_of(pos_ref[0, 0, k * tm + r] * sub, sub)
        return pltpu.make_async_copy(y_hbm.at[pl.ds(p, sub)], ybuf.at[s, k, pl.ds(r * sub, sub)], sem.at[s])

    def start_gather(pos_ref, s):
        for k in range(K):
            for r in range(tm):
                row_copy(pos_ref, k, r, s).start(priority=r % 2)

    @pl.when(j == 0)
    def _():
        start_gather(pos_cur_ref, 0)

    @pl.when(j + 1 < nt)
    def _():
        start_gather(pos_nxt_ref, 1 - slot)

    for k in range(K):
        pltpu.make_async_copy(y_hbm.at[pl.ds(0, tm * sub)], ybuf.at[slot, k], sem.at[slot]).wait()

    def routed(k):
        rows = ybuf.at[slot, k]
        return jnp.concatenate([rows[pl.ds(i, tm, stride=sub), :] for i in range(sub)], axis=1)

    w = wts_ref[...]
    acc = routed(0) * w[:, 0:1]
    for k in range(1, K):
        acc = acc + routed(k) * w[:, k:k + 1]
    hb = hin_ref[...].astype(jnp.bfloat16)
    s13 = jnp.dot(hb, sw13_ref[...], preferred_element_type=jnp.float32)
    sh = (jax.nn.silu(s13[:, :ds]) * s13[:, ds:]).astype(jnp.bfloat16)
    acc = acc + jnp.dot(sh, sw2_ref[...], preferred_element_type=jnp.float32)
    h = hres_ref[...] + gt_ref[0] * acc
    ms = jnp.mean(h * h, axis=-1, keepdims=True)
    out_ref[...] = h * lax.rsqrt(ms + NORM_EPS) * fg_ref[...]


def moe_combine_norm(y_rows, pos, wts, h_in, h_res, gate, sw13, sw2, final_g, seq):
    N, D = h_in.shape
    K = pos.shape[1]
    lanes = y_rows.shape[1]
    tm = COMBINE_ROWS
    nt = N // tm
    tiles_per_seq = seq // tm
    pos3 = pos.reshape(nt, tm, K).transpose(0, 2, 1).reshape(nt, 1, K * tm)
    last = nt - 1
    return pl.pallas_call(
        _combine_body,
        grid=(nt,),
        in_specs=[
            pl.BlockSpec((1, 1, K * tm), lambda j: (j, 0, 0), memory_space=pltpu.SMEM),
            pl.BlockSpec((1, 1, K * tm), lambda j: (jnp.minimum(j + 1, last), 0, 0), memory_space=pltpu.SMEM),
            pl.BlockSpec(memory_space=pl.ANY),
            pl.BlockSpec((tm, K), lambda j: (j, 0)),
            pl.BlockSpec((tm, D), lambda j: (j, 0)),
            pl.BlockSpec((tm, D), lambda j: (j, 0)),
            pl.BlockSpec((1, 1, D), lambda j: (j // tiles_per_seq, 0, 0)),
            pl.BlockSpec(sw13.shape, lambda j: (0, 0)),
            pl.BlockSpec(sw2.shape, lambda j: (0, 0)),
            pl.BlockSpec((1, D), lambda j: (0, 0)),
        ],
        out_specs=pl.BlockSpec((tm, D), lambda j: (j, 0)),
        out_shape=jax.ShapeDtypeStruct((N, D), jnp.float32),
        scratch_shapes=[pltpu.VMEM((2, K, tm * (D // lanes), lanes), jnp.float32), pltpu.SemaphoreType.DMA((2,))],
        compiler_params=pltpu.CompilerParams(dimension_semantics=("arbitrary",), vmem_limit_bytes=VMEM_LIMIT),
    )(pos3, pos3, y_rows, wts, h_in, h_res, gate.reshape(-1, 1, D), sw13, sw2, final_g.reshape(1, D))


def moe_block_final(h_in, h_slab, h_res, gate, top_e, wts, w1, w3, w2, sw1, sw3, sw2, final_g, seq):
    bf = jnp.bfloat16
    blk_e, n_valid_blk, row_tok, pos = moe_layout(top_e, w1.shape[0])
    w13 = jnp.concatenate([w1, w3], axis=-1).astype(bf)
    y_rows = expert_ffn_rows(h_slab, blk_e, n_valid_blk, row_tok, w13, w2.astype(bf))
    sw13 = jnp.concatenate([sw1, sw3], axis=-1).astype(bf)
    return moe_combine_norm(y_rows, pos, wts, h_in, h_res, gate, sw13, sw2.astype(bf), final_g, seq)


def kernel(x, c, ctx, c_ctx, w_ada, b_ada, norm1_g, w_in, mu_shift, w0, w_up, a0, a_up, g_up,
           k_k, k_a, r_k, lnx_w, lnx_b, conv_w, A_log, dt_bias, onorm_g, w_out, norm2_g,
           router_w, router_b, exp_w1, exp_w3, exp_w2, sh_w1, sh_w3, sh_w2, final_g):
    h_x, h_c = x, ctx
    l = 0
    mod = jax.nn.silu(c) @ w_ada[l] + b_ada[l]
    mod_c = jax.nn.silu(c_ctx) @ w_ada[l] + b_ada[l]
    sh1, sc1, gt1, sh2, sc2, gt2 = (m[:, None] for m in jnp.split(mod, 6, axis=-1))
    sh1c, sc1c, gt1c, sh2c, sc2c, gt2c = jnp.split(mod_c, 6, axis=-1)

    hm_x = rmsnorm(h_x, norm1_g[l]) * (1.0 + sc1) + sh1
    hm_c = rmsnorm(h_c, norm1_g[l]) * (1.0 + sc1c) + sh1c
    w_in_a, w_in_b = w_in[l][:, :RWKV_COLS], w_in[l][:, RWKV_COLS:]
    ya_x = rwkv7_mixer_latent(hm_c @ w_in_a, hm_x @ w_in_a, mu_shift[l], w0[l], w_up[l],
                              a0[l], a_up[l], g_up[l], k_k[l], k_a[l], r_k[l], lnx_w[l], lnx_b[l])
    yb_x = gdn_mixer_latent(hm_c @ w_in_b, hm_x @ w_in_b, conv_w[l], A_log[l], dt_bias[l], onorm_g[l])
    B, T, D = h_x.shape
    N = B * T
    h_res, h_in, h_slab, top_e, wts = mixer_out_route(
        ya_x.reshape(N, DA), yb_x.reshape(N, DB), x.reshape(N, D), gt1[:, 0], sc2[:, 0], sh2[:, 0],
        w_out[l], norm2_g[l], router_w[l], router_b[l], T)
    out = moe_block_final(h_in, h_slab, h_res, gt2[:, 0], top_e.T, wts.T,
                          exp_w1[l], exp_w3[l], exp_w2[l], sh_w1[l], sh_w3[l], sh_w2[l], final_g, T)
    return out.reshape(B, T, D)
```

```python
import functools

import jax
import jax.numpy as jnp
from jax import lax
from jax.experimental import pallas as pl
from jax.experimental.pallas import tpu as pltpu

D_MODEL = 1024
BATCH = 32
SEQ = 2048
DEPTH = 1
CTX_LEN = 256
GRID_W = 64
D_MIX = D_MODEL
DA = D_MIX // 2
NA = 64
HA = DA // NA
DECAY_LORA = 64
AAA_LORA = 64
GATE_LORA = 128
RWKV_COLS = 3 * DA + 2 * DECAY_LORA + 2 * AAA_LORA + GATE_LORA
RWKV_EPS = 64e-5
DB = D_MIX - DA
HB = 4
KB = DB // HB
GDN_CONV = 5
GDN_CHUNK = 64
GDN_COLS = 4 * DB + 4 * HB
IN_COLS = RWKV_COLS + GDN_COLS
N_EXPERTS = 256
TOP_K = 8
N_GROUPS = 8
TOPK_GROUPS = 4
D_EXPERT = 256
D_SHARED = 256
ROUTED_SCALE = 2.5
MOE_BLOCK = 256
NORM_EPS = 1e-6


def rmsnorm(x, g):
    xf = x.astype(jnp.float32)
    y = xf * lax.rsqrt(jnp.mean(xf * xf, axis=-1, keepdims=True) + NORM_EPS)
    return (y * g.astype(jnp.float32)).astype(x.dtype)


def l2norm(x):
    xf = x.astype(jnp.float32)
    return xf * lax.rsqrt(jnp.sum(xf * xf, axis=-1, keepdims=True) + 1e-6)


def shift_seq(p):
    h = p.shape[-1] // 2
    prev = jnp.pad(p[:, :-1, :h], ((0, 0), (1, 0), (0, 0)))
    nxt = jnp.pad(p[:, 1:, h:], ((0, 0), (0, 1), (0, 0)))
    return jnp.concatenate([prev, nxt], axis=-1)


def shift_grid(p):
    B, T, C = p.shape
    rows = T // GRID_W
    q = C // 4
    g = p.reshape(B, rows, GRID_W, C)
    left = jnp.pad(g[:, :, :-1, :q], ((0, 0), (0, 0), (1, 0), (0, 0)))
    right = jnp.pad(g[:, :, 1:, q:2 * q], ((0, 0), (0, 0), (0, 1), (0, 0)))
    up = jnp.pad(g[:, :-1, :, 2 * q:3 * q], ((0, 0), (1, 0), (0, 0), (0, 0)))
    down = jnp.pad(g[:, 1:, :, 3 * q:], ((0, 0), (0, 1), (0, 0), (0, 0)))
    return jnp.concatenate([left, right, up, down], axis=-1).reshape(B, T, C)


def dwconv_centered(x, w):
    C = x.shape[-1]
    return lax.conv_general_dilated(
        x, w[:, None, :].astype(x.dtype), window_strides=(1,),
        padding=[(GDN_CONV // 2, GDN_CONV // 2)],
        dimension_numbers=('NWC', 'WIO', 'NWC'), feature_group_count=C)


def gdn_prepare(p, conv_w, A_log, dt_bias):
    B, T, _ = p.shape
    qkv = jax.nn.silu(dwconv_centered(p[..., :3 * DB], conv_w)).astype(jnp.float32)
    q, k, v = jnp.split(qkv, 3, axis=-1)
    q = l2norm(q.reshape(B, T, HB, KB)) * (KB ** -0.5)
    k = l2norm(k.reshape(B, T, HB, KB))
    v = v.reshape(B, T, HB, KB)
    z = p[..., 3 * DB:4 * DB]
    gl = p[..., 4 * DB:].astype(jnp.float32).reshape(B, T, 2, 2, HB)
    g = -jnp.exp(A_log) * jax.nn.softplus(gl[:, :, 0] + dt_bias)
    beta = jax.nn.sigmoid(gl[:, :, 1])
    return q, k, v, g, beta, z


WKV_CHUNK = 64
WKV_GROUP = 4
WKV_WIDTH = WKV_GROUP * NA
WKV_ROWS = 2


def _wkv_chunk_body(r_ref, v_ref, kk_ref, lw_ref, k_ref, b_ref, y_ref, st_ref):
    W = WKV_WIDTH
    fwd = pl.program_id(0) == 0

    @pl.when(pl.program_id(2) == 0)
    def _():
        st_ref[...] = jnp.zeros_like(st_ref)

    rows, n_groups = r_ref.shape[0], r_ref.shape[2] // W
    where = [(i, slice(g * W, (g + 1) * W)) for i in range(rows) for g in range(n_groups)]
    chains = [_wkv_chunk_math(fwd, r_ref[i, :, lanes], v_ref[i, :, lanes], kk_ref[i, :, lanes],
                              lw_ref[0, i, :, lanes], k_ref[0, i, :, lanes], b_ref[0, i, :, lanes], st_ref[s])
              for s, (i, lanes) in enumerate(where)]
    for s, (y, st) in enumerate(_interleave(chains)):
        i, lanes = where[s]
        y_ref[0, i, :, lanes] = y
        st_ref[s] = st


def _interleave(chains):
    results = [None] * len(chains)
    live = list(range(len(chains)))
    while live:
        for i in list(live):
            try:
                next(chains[i])
            except StopIteration as stop:
                results[i] = stop.value
                live.remove(i)
    return results


def _wkv_chunk_math(fwd, r, v, kk, lw, k, b, st):
    C, W, G = WKV_CHUNK, WKV_WIDTH, WKV_GROUP
    f32, bf16 = jnp.float32, jnp.bfloat16
    sgn = jnp.where(fwd, 1, -1)

    row = lax.broadcasted_iota(jnp.int32, (C, C), 0)
    col = lax.broadcasted_iota(jnp.int32, (C, C), 1)
    tri = jnp.where((col - row) * sgn <= 0, 1.0, 0.0).astype(bf16)
    hi = lw.astype(bf16)
    rem = lw - hi.astype(f32)
    mid = rem.astype(bf16)
    lo = (rem - mid.astype(f32)).astype(bf16)
    dot = functools.partial(jnp.dot, preferred_element_type=f32)
    cinc = dot(tri, hi) + dot(tri, mid) + dot(tri, lo)
    yield
    ctot = jnp.where(fwd, cinc[C - 1:C], cinc[0:1])
    e_out = jnp.exp(-cinc)
    e_last = jnp.exp(ctot - cinc)

    rr = lax.broadcasted_iota(jnp.int32, (G * C, W), 0)
    cc = lax.broadcasted_iota(jnp.int32, (G * C, W), 1)
    same_head = (rr // C) == (cc // NA)

    def bdiag(x):
        return jnp.where(same_head, jnp.concatenate([x] * G, axis=0), 0.0).astype(bf16)

    lhs = jnp.concatenate([bdiag(kk * jnp.exp(cinc - lw)), bdiag(r * jnp.exp(cinc))], axis=0)
    rhs = jnp.concatenate([bdiag(b * e_out), bdiag(k * e_out)], axis=0)
    trans_b = (((1,), (1,)), ((), ()))
    gram = lax.dot_general(lhs, rhs, trans_b, preferred_element_type=f32)
    hs = lax.dot_general(lhs, st.astype(bf16), trans_b, preferred_element_type=f32)
    yield

    tr, tc = rr % C, cc % C
    order = (tc - tr) * sgn
    strict = order < 0
    incl = order <= 0
    n = G * C
    a_bk = jnp.where(strict, gram[:n, :n], 0.0)
    a_kk = jnp.where(strict, gram[:n, n:], 0.0)
    rb = jnp.where(incl, gram[n:, :n], 0.0)
    rk = jnp.where(incl, gram[n:, n:], 0.0)

    vbd = bdiag(v)
    x = hs[:n] + dot(a_kk.astype(bf16), vbd)
    p = -a_bk
    steps = C.bit_length() - 1
    for i in range(steps):
        yield
        pb = p.astype(bf16)
        x = x + dot(pb, x.astype(bf16))
        if i + 1 < steps:
            p = dot(pb, pb)
    ub = x.astype(bf16)
    yield

    y = hs[n:] + dot(jnp.concatenate([rk, -rb], axis=1).astype(bf16), jnp.concatenate([vbd, ub], axis=0))
    yield

    trans_a = (((0,), (0,)), ((), ()))
    upd = lax.dot_general(jnp.concatenate([vbd, -ub], axis=0),
                          jnp.concatenate([bdiag(k * e_last), bdiag(b * e_last)], axis=0),
                          trans_a, preferred_element_type=f32)
    return sum(y[h * C:(h + 1) * C] for h in range(G)), st * jnp.exp(ctot) + upd


def wkv7_chunked(r, v, kk, lw, k_dir, b_dir, n_ctx_chunks):
    B, T, da = r.shape
    C, W = WKV_CHUNK, WKV_WIDTH
    n = T // C

    def chunk(d, c):
        back = jnp.where(c < n_ctx_chunks, n_ctx_chunks - 1 - c, n + n_ctx_chunks - 1 - c)
        return jnp.where(d == 0, c, back)

    rows = WKV_ROWS
    shared = pl.BlockSpec((rows, C, da), lambda d, bi, c: (bi, chunk(d, c), 0))
    per_dir = pl.BlockSpec((1, rows, C, da), lambda d, bi, c: (d, bi, chunk(d, c), 0))
    return pl.pallas_call(
        _wkv_chunk_body,
        grid=(2, B // rows, n),
        in_specs=[shared, shared, shared, per_dir, per_dir, per_dir],
        out_specs=per_dir,
        out_shape=jax.ShapeDtypeStruct((2, B, T, da), jnp.float32),
        scratch_shapes=[pltpu.VMEM((rows * (da // W), W, W), jnp.float32)],
        compiler_params=pltpu.CompilerParams(dimension_semantics=("arbitrary",) * 3,
                                             vmem_limit_bytes=VMEM_LIMIT),
    )(r, v, kk, lw, k_dir, b_dir)


def rwkv7_prepare_bm(p, shifted, mu, w0, w_up, a0, a_up, g_up, k_k, k_a):
    p = (p + mu * (shifted - p)).astype(jnp.float32)
    B, T = p.shape[:2]
    cuts = [DA, 2 * DA, 3 * DA, 3 * DA + 2 * DECAY_LORA, 3 * DA + 2 * DECAY_LORA + 2 * AAA_LORA]
    r, k, v, wl, al, gl = jnp.split(p, cuts, axis=-1)
    wl = wl.reshape(B, T, 2, DECAY_LORA)
    al = al.reshape(B, T, 2, AAA_LORA)
    w_log = -jax.nn.softplus(-(w0[:, None, None] + jnp.einsum('btdl,dlc->dbtc', jnp.tanh(wl), w_up))) - 0.5
    lw = -jnp.exp(w_log)
    a = jax.nn.sigmoid(a0[:, None, None] + jnp.einsum('btdl,dlc->dbtc', al, a_up))
    g = jax.nn.sigmoid(gl) @ g_up
    kk = l2norm((k * k_k).reshape(B, T, HA, NA)).reshape(B, T, DA)
    k_dir = k[None] * (1.0 + (a - 1.0) * k_a)
    b_dir = kk[None] * a
    return r, v, kk, g, lw, k_dir, b_dir


def rwkv7_mixer_latent(p_ctx, p_lat, mu, w0, w_up, a0, a_up, g_up, k_k, k_a, r_k, lnx_w, lnx_b):
    t_ctx = p_ctx.shape[1]
    p = jnp.concatenate([p_ctx, p_lat], axis=1)
    shifted = jnp.concatenate([shift_seq(p_ctx), shift_grid(p_lat)], axis=1)
    r, v, kk, g, lw, k_dir, b_dir = rwkv7_prepare_bm(p, shifted, mu, w0, w_up, a0, a_up, g_up, k_k, k_a)
    y2 = wkv7_chunked(r, v, kk, lw, k_dir, b_dir, t_ctx // WKV_CHUNK)
    sl = lambda t: t[..., t_ctx:, :]
    r, v, g, k_dir = sl(r), sl(v), sl(g), sl(k_dir)
    B, T = r.shape[:2]
    y = (sl(y2[0]) + sl(y2[1])).reshape(B, T, HA, NA)
    m = jnp.mean(y, axis=-1, keepdims=True)
    var = jnp.mean(jnp.square(y - m), axis=-1, keepdims=True)
    y = ((y - m) * lax.rsqrt(var + RWKV_EPS)).reshape(B, T, DA) * lnx_w + lnx_b
    heads = lambda t: t.reshape(t.shape[:-1] + (HA, NA))
    bonus = jnp.sum(heads(r)[None] * heads(k_dir) * r_k, axis=(0, 4))
    return (y + (bonus[..., None] * heads(v)).reshape(B, T, DA)) * g


GDN_ROWS = 4


def _gdn_chunk_body(q_ref, k_ref, v_ref, gc_ref, grow_ref, beta_ref, o_ref, st_ref):
    fwd = pl.program_id(0) == 0

    @pl.when(pl.program_id(2) == 0)
    def _():
        st_ref[...] = jnp.zeros_like(st_ref)

    def over_lanes(a):
        return jnp.concatenate([jnp.broadcast_to(a[:, h:h + 1], (a.shape[0], KB)) for h in range(HB)], axis=1)

    chains = [_gdn_chunk_math(fwd, q_ref[i], k_ref[i], v_ref[i], over_lanes(gc_ref[0, i]), grow_ref[0, i, 0],
                              over_lanes(beta_ref[0, i]), st_ref[i]) for i in range(q_ref.shape[0])]
    for i, (o, st) in enumerate(_interleave(chains)):
        o_ref[0, i] = o
        st_ref[i] = st


def _gdn_chunk_math(fwd, q, k, v, gcx, grow, bx, st):
    C, H, KD = GDN_CHUNK, HB, KB
    W, n = H * KD, H * C
    f32, bf16 = jnp.float32, jnp.bfloat16
    sgn = jnp.where(fwd, 1, -1)
    dot = functools.partial(jnp.dot, preferred_element_type=f32)
    head = lambda x, h: x[:, h * KD:(h + 1) * KD]

    rr = lax.broadcasted_iota(jnp.int32, (n, W), 0)
    cc = lax.broadcasted_iota(jnp.int32, (n, W), 1)
    same_head = (rr // C) == (cc // KD)

    def bdiag(x):
        return jnp.where(same_head, jnp.concatenate([x] * H, axis=0), 0.0).astype(bf16)

    eg = jnp.exp(gcx)
    kb = k * bx
    trans_b = (((1,), (1,)), ((), ()))
    gram = lax.dot_general(jnp.concatenate([bdiag(kb), bdiag(q)], axis=0), bdiag(k), trans_b,
                           preferred_element_type=f32)
    yield

    r2 = lax.broadcasted_iota(jnp.int32, (n, n), 0)
    c2 = lax.broadcasted_iota(jnp.int32, (n, n), 1)
    order = (c2 % C - r2 % C) * sgn
    same2 = (r2 // C) == (c2 // C)
    incl = same2 & (order <= 0)
    strict = same2 & (order < 0)
    g_rows = jnp.concatenate([jnp.concatenate([head(gcx, h)] * (n // KD), axis=1) for h in range(H)], axis=0)
    decay = jnp.exp(jnp.where(incl, g_rows - grow, 0.0))
    a_low = jnp.where(strict, gram[:n] * decay, 0.0)
    attn = jnp.where(incl, gram[n:] * decay, 0.0)

    vb, kbe = v * bx, kb * eg
    x = jnp.concatenate([jnp.concatenate([head(vb, h), head(kbe, h)], axis=1) for h in range(H)], axis=0)
    p = -a_low
    steps = C.bit_length() - 1
    for i in range(steps):
        yield
        pb = p.astype(bf16)
        x = x + dot(pb, x.astype(bf16))
        if i + 1 < steps:
            p = dot(pb, pb)
    u, w = x[:, :KD], x[:, KD:]
    yield

    stb = st.astype(bf16)
    w_bd = jnp.where(same_head, jnp.concatenate([w] * H, axis=1), 0.0).astype(bf16)
    v_new = (u - dot(w_bd, stb)).astype(bf16)
    yield
    o = dot(jnp.concatenate([bdiag(q * eg), attn.astype(bf16)], axis=1), jnp.concatenate([stb, v_new], axis=0))
    yield

    g_last = jnp.where(fwd, gcx[C - 1:C], gcx[0:1])
    trans_a = (((0,), (0,)), ((), ()))
    upd = lax.dot_general(bdiag(k * jnp.exp(g_last - gcx)), v_new, trans_a, preferred_element_type=f32)
    e_last = jnp.exp(g_last)
    st_new = jnp.concatenate([st[h * KD:(h + 1) * KD] * head(e_last, h) for h in range(H)], axis=0) + upd
    return jnp.concatenate([o[h * C:(h + 1) * C] for h in range(H)], axis=1), st_new


def gdn_chunked_pallas(q, k, v, gc, grow, beta, n_ctx_chunks):
    B, T, W = q.shape
    C = GDN_CHUNK
    n = T // C

    def chunk(d, c):
        back = jnp.where(c < n_ctx_chunks, n_ctx_chunks - 1 - c, n + n_ctx_chunks - 1 - c)
        return jnp.where(d == 0, c, back)

    nb = GDN_ROWS
    shared = pl.BlockSpec((nb, C, W), lambda d, bi, c: (bi, chunk(d, c), 0))
    per_dir = pl.BlockSpec((1, nb, C, W), lambda d, bi, c: (d, bi, chunk(d, c), 0))
    rows = pl.BlockSpec((1, nb, 1, 1, HB * C), lambda d, bi, c: (d, bi, chunk(d, c), 0, 0))
    per_head = pl.BlockSpec((1, nb, C, HB), lambda d, bi, c: (d, bi, chunk(d, c), 0))
    return pl.pallas_call(
        _gdn_chunk_body,
        grid=(2, B // nb, n),
        in_specs=[shared, shared, shared, per_head, rows, per_head],
        out_specs=per_dir,
        out_shape=jax.ShapeDtypeStruct((2, B, T, W), jnp.float32),
        scratch_shapes=[pltpu.VMEM((nb, W, KB), jnp.float32)],
        compiler_params=pltpu.CompilerParams(dimension_semantics=("arbitrary",) * 3,
                                             vmem_limit_bytes=VMEM_LIMIT),
    )(q, k, v, gc, grow, beta)


def gdn_mixer_latent(p_ctx, p_lat, conv_w, A_log, dt_bias, onorm_g):
    t_ctx = p_ctx.shape[1]
    qc, kc, vc, gc_, bc, _ = gdn_prepare(p_ctx, conv_w, A_log, dt_bias)
    ql, kl, vl, gl_, bl, z = gdn_prepare(p_lat, conv_w, A_log, dt_bias)
    cat = lambda a, b: jnp.concatenate([a, b], axis=1)
    B, T = ql.shape[0], t_ctx + ql.shape[1]
    C = GDN_CHUNK
    flat = lambda t: t.reshape(B, T, DB)
    q, k, v = flat(cat(qc, ql)), flat(cat(kc, kl)), flat(cat(vc, vl))
    g = jnp.moveaxis(cat(gc_, gl_), 2, 0).reshape(2, B, T // C, C, HB)
    gcs = jnp.stack([jnp.cumsum(g[0], axis=2), lax.cumsum(g[1], axis=2, reverse=True)])
    grow = jnp.swapaxes(gcs, 3, 4).reshape(2, B, T // C, 1, HB * C)
    beta = jnp.moveaxis(cat(bc, bl), 2, 0)
    o2 = gdn_chunked_pallas(q, k, v, gcs.reshape(2, B, T, HB), grow, beta, t_ctx // C)
    o = (o2[0, :, t_ctx:] + o2[1, :, t_ctx:]).reshape(B, T - t_ctx, HB, KB)
    o = o * lax.rsqrt(jnp.mean(o * o, axis=-1, keepdims=True) + NORM_EPS) * onorm_g
    return o.reshape(B, T - t_ctx, DB) * jax.nn.silu(z.astype(jnp.float32))


LANES = 128
ROW_BLOCK = 512
COMBINE_ROWS = 128
LAYOUT_TOKENS = 256
VMEM_LIMIT = 48 * 1024 * 1024


def moe_route(xs, router_w, router_b):
    N = xs.shape[0]
    E = router_w.shape[1]
    scores = jax.nn.sigmoid((xs @ router_w).astype(jnp.float32))
    sel = scores + router_b.astype(jnp.float32)
    grp_score = jnp.sum(lax.top_k(sel.reshape(N, N_GROUPS, E // N_GROUPS), 2)[0], axis=-1)
    top_g = lax.top_k(grp_score, TOPK_GROUPS)[1]
    gmask = jnp.any(top_g[:, :, None] == jnp.arange(N_GROUPS)[None, None, :], axis=1)
    emask = jnp.repeat(gmask, E // N_GROUPS, axis=1)
    top_e = lax.top_k(jnp.where(emask, sel, -jnp.inf), TOP_K)[1]
    wts = jnp.take_along_axis(scores, top_e, axis=1)
    wts = wts / jnp.sum(wts, axis=-1, keepdims=True) * ROUTED_SCALE
    return top_e, wts


MID_ROWS = 256


def _first_max(x, idx, axis_len):
    m = jnp.max(x, axis=0, keepdims=True)
    i = jnp.min(jnp.where(x == m, idx, axis_len), axis=0, keepdims=True)
    return m, i


def _mid_body(ya_ref, yb_ref, x_ref, gt1_ref, sc2_ref, sh2_ref, wo_ref, g2_ref, rwt_ref, rb_ref,
              hx_ref, hin_ref, slab_ref, te_ref, wt_ref):
    f32, bf16 = jnp.float32, jnp.bfloat16
    half = ya_ref.shape[1]
    mix = (jnp.dot(ya_ref[...].astype(bf16), wo_ref[:half], preferred_element_type=f32)
           + jnp.dot(yb_ref[...].astype(bf16), wo_ref[half:], preferred_element_type=f32))
    hx = x_ref[...] + gt1_ref[0] * mix
    hx_ref[...] = hx
    ms = jnp.mean(hx * hx, axis=-1, keepdims=True)
    hin = hx * lax.rsqrt(ms + NORM_EPS) * g2_ref[...] * (1.0 + sc2_ref[0]) + sh2_ref[0]
    hin_ref[...] = hin
    lanes = slab_ref.shape[1]
    sub = hin.shape[1] // lanes
    for j in range(sub):
        slab_ref[pl.ds(j, hin.shape[0], stride=sub), :] = hin[:, j * lanes:(j + 1) * lanes]

    logits = lax.dot_general(rwt_ref[...], hin.astype(bf16), (((1,), (1,)), ((), ())),
                             preferred_element_type=f32)
    n_e, tm = logits.shape
    per_group = n_e // N_GROUPS
    scores = jax.nn.sigmoid(logits)
    sel = scores + rb_ref[...]
    neg = -jnp.inf

    lidx = lax.broadcasted_iota(jnp.int32, (per_group, tm), 0)
    gs = []
    for g in range(N_GROUPS):
        blk = sel[g * per_group:(g + 1) * per_group]
        m1, i1 = _first_max(blk, lidx, per_group)
        m2 = jnp.max(jnp.where(lidx == i1, neg, blk), axis=0, keepdims=True)
        gs.append(m1 + m2)
    gsc = jnp.concatenate(gs, axis=0)

    gidx = lax.broadcasted_iota(jnp.int32, (N_GROUPS, tm), 0)
    chosen = jnp.zeros((N_GROUPS, tm), f32)
    for _ in range(TOPK_GROUPS):
        _, gi = _first_max(gsc, gidx, N_GROUPS)
        hit = gidx == gi
        chosen = jnp.where(hit, 1.0, chosen)
        gsc = jnp.where(hit, neg, gsc)
    allowed = jnp.concatenate([jnp.broadcast_to(chosen[g:g + 1], (per_group, tm)) for g in range(N_GROUPS)], axis=0)
    masked = jnp.where(allowed > 0.5, sel, neg)

    eidx = lax.broadcasted_iota(jnp.int32, (n_e, tm), 0)
    ids, ws = [], []
    for _ in range(TOP_K):
        _, ei = _first_max(masked, eidx, n_e)
        hit = eidx == ei
        ids.append(ei)
        ws.append(jnp.sum(jnp.where(hit, scores, 0.0), axis=0, keepdims=True))
        masked = jnp.where(hit, neg, masked)
    w = jnp.concatenate(ws, axis=0)
    te_ref[...] = jnp.concatenate(ids, axis=0)
    wt_ref[...] = w / jnp.sum(w, axis=0, keepdims=True) * ROUTED_SCALE


def mixer_out_route(ya, yb, x, gt1, sc2, sh2, w_out, norm2_g, router_w, router_b, seq):
    N, D = x.shape
    sub = D // LANES
    E = router_w.shape[1]
    tm = MID_ROWS
    per_seq = seq // tm
    row = lambda i: (i, 0)
    per_batch = pl.BlockSpec((1, 1, D), lambda i: (i // per_seq, 0, 0))
    whole = lambda a: pl.BlockSpec(a.shape, lambda i: (0,) * a.ndim)
    wo = w_out.astype(jnp.bfloat16)
    rwt = router_w.T.astype(jnp.bfloat16)
    g2 = norm2_g.reshape(1, D)
    rb = router_b.reshape(E, 1).astype(jnp.float32)
    b3 = lambda a: a.reshape(-1, 1, D)
    return pl.pallas_call(
        _mid_body,
        grid=(N // tm,),
        in_specs=[pl.BlockSpec((tm, D // 2), row), pl.BlockSpec((tm, D // 2), row), pl.BlockSpec((tm, D), row),
                  per_batch, per_batch, per_batch, whole(wo), whole(g2), whole(rwt), whole(rb)],
        out_specs=[pl.BlockSpec((tm, D), row), pl.BlockSpec((tm, D), row), pl.BlockSpec((tm * sub, LANES), row),
                   pl.BlockSpec((TOP_K, tm), lambda i: (0, i)), pl.BlockSpec((TOP_K, tm), lambda i: (0, i))],
        out_shape=[jax.ShapeDtypeStruct((N, D), jnp.float32), jax.ShapeDtypeStruct((N, D), jnp.float32),
                   jax.ShapeDtypeStruct((N * sub, LANES), jnp.float32),
                   jax.ShapeDtypeStruct((TOP_K, N), jnp.int32), jax.ShapeDtypeStruct((TOP_K, N), jnp.float32)],
        compiler_params=pltpu.CompilerParams(dimension_semantics=("arbitrary",), vmem_limit_bytes=VMEM_LIMIT),
    )(ya, yb, x, b3(gt1), b3(sc2), b3(sh2), wo, g2, rwt, rb)


def moe_layout(top_e, n_experts):
    N, K = top_e.shape
    nk = N * K
    tb = LAYOUT_TOKENS
    eids = jnp.arange(n_experts, dtype=jnp.int32)
    hit = top_e[:, :, None] == eids[None, None, :]
    used = jnp.any(hit, axis=1).astype(jnp.bfloat16).reshape(N // tb, tb, n_experts)
    before = (jnp.arange(tb)[:, None] > jnp.arange(tb)[None, :]).astype(jnp.bfloat16)
    within = jnp.einsum('ij,bje->bie', before, used, preferred_element_type=jnp.float32)
    blk_tot = jnp.sum(used.astype(jnp.float32), axis=1)
    blk_before = jnp.cumsum(blk_tot, axis=0) - blk_tot
    rank = (within + blk_before[:, None, :]).reshape(N, n_experts).astype(jnp.int32)
    cnt = jnp.sum(blk_tot, axis=0).astype(jnp.int32)
    padded = (cnt + ROW_BLOCK - 1) // ROW_BLOCK * ROW_BLOCK
    pend = jnp.cumsum(padded).astype(jnp.int32)
    row0 = rank + (pend - padded)[None, :]
    pos = jnp.sum(jnp.where(hit, row0[:, None, :], 0), axis=2)
    n_blk = (nk + n_experts * (ROW_BLOCK - 1) + ROW_BLOCK - 1) // ROW_BLOCK
    tok = jnp.broadcast_to(jnp.arange(N, dtype=jnp.int32)[:, None], (N, K))
    row_tok = jnp.zeros((n_blk * ROW_BLOCK,), jnp.int32).at[pos.reshape(nk)].set(tok.reshape(nk))
    blk_start = jnp.arange(n_blk, dtype=jnp.int32) * ROW_BLOCK
    blk_e = jnp.minimum(jnp.searchsorted(pend, blk_start, side='right'), n_experts - 1).astype(jnp.int32)
    n_valid_blk = (pend[-1] // ROW_BLOCK).astype(jnp.int32).reshape(1)
    return blk_e, n_valid_blk, row_tok, pos


def _expert_ffn_body(blk_e_ref, nvb_ref, tok_cur_ref, tok_nxt_ref, x_hbm, w13_ref, w2_ref, y_ref, xbuf, sem):
    i = pl.program_id(0)
    nvb = nvb_ref[0]
    slot = lax.rem(i, 2)
    de = w2_ref.shape[1]
    sub = w13_ref.shape[1] // x_hbm.shape[1]

    def row_copy(tok_ref, r, s):
        t = pl.multiple_of(tok_ref[0, 0, r] * sub, sub)
        return pltpu.make_async_copy(x_hbm.at[pl.ds(t, sub)], xbuf.at[s, pl.ds(r * sub, sub)], sem.at[s])

    def start_gather(tok_ref, s):
        for r in range(ROW_BLOCK):
            row_copy(tok_ref, r, s).start(priority=r % 2)

    @pl.when((i == 0) & (nvb > 0))
    def _():
        start_gather(tok_cur_ref, 0)

    @pl.when(i + 1 < nvb)
    def _():
        start_gather(tok_nxt_ref, 1 - slot)

    @pl.when(i < nvb)
    def _():
        pltpu.make_async_copy(x_hbm.at[pl.ds(0, ROW_BLOCK * sub)], xbuf.at[slot], sem.at[slot]).wait()
        xb = jnp.concatenate([xbuf[slot, pl.ds(j, ROW_BLOCK, stride=sub), :] for j in range(sub)],
                             axis=1).astype(jnp.bfloat16)
        h13 = jnp.dot(xb, w13_ref[0], preferred_element_type=jnp.float32)
        h = (jax.nn.silu(h13[:, :de]) * h13[:, de:]).astype(jnp.bfloat16)
        y = jnp.dot(h, w2_ref[0], preferred_element_type=jnp.float32)
        lanes = y_ref.shape[1]
        for j in range(sub):
            y_ref[pl.ds(j, ROW_BLOCK, stride=sub), :] = y[:, j * lanes:(j + 1) * lanes]

    @pl.when(i >= nvb)
    def _():
        y_ref[...] = jnp.zeros_like(y_ref)


def expert_ffn_rows(xs, blk_e, n_valid_blk, row_tok, w13, w2):
    D = w13.shape[1]
    lanes = xs.shape[1]
    n_blk = blk_e.shape[0]
    de2 = w13.shape[2]
    tok3 = row_tok.reshape(n_blk, 1, ROW_BLOCK)
    last = n_blk - 1
    grid_spec = pltpu.PrefetchScalarGridSpec(
        num_scalar_prefetch=2,
        grid=(n_blk,),
        in_specs=[
            pl.BlockSpec((1, 1, ROW_BLOCK), lambda i, be, nv: (i, 0, 0), memory_space=pltpu.SMEM),
            pl.BlockSpec((1, 1, ROW_BLOCK), lambda i, be, nv: (jnp.minimum(i + 1, last), 0, 0),
                         memory_space=pltpu.SMEM),
            pl.BlockSpec(memory_space=pl.ANY),
            pl.BlockSpec((1, D, de2), lambda i, be, nv: (be[i], 0, 0)),
            pl.BlockSpec((1, de2 // 2, D), lambda i, be, nv: (be[i], 0, 0)),
        ],
        out_specs=pl.BlockSpec((ROW_BLOCK * (D // lanes), lanes), lambda i, be, nv: (i, 0)),
        scratch_shapes=[pltpu.VMEM((2, ROW_BLOCK * (D // lanes), lanes), jnp.float32),
                        pltpu.SemaphoreType.DMA((2,))],
    )
    return pl.pallas_call(
        _expert_ffn_body,
        grid_spec=grid_spec,
        out_shape=jax.ShapeDtypeStruct((n_blk * ROW_BLOCK * (D // lanes), lanes), jnp.float32),
        compiler_params=pltpu.CompilerParams(dimension_semantics=("arbitrary",), vmem_limit_bytes=VMEM_LIMIT),
    )(blk_e, n_valid_blk, tok3, tok3, xs, w13, w2)


def _combine_body(pos_cur_ref, pos_nxt_ref, y_hbm, wts_ref, hin_ref, hres_ref, gt_ref, sw13_ref, sw2_ref,
                  fg_ref, out_ref, ybuf, sem):
    j = pl.program_id(0)
    nt = pl.num_programs(0)
    slot = lax.rem(j, 2)
    K = wts_ref.shape[1]
    tm = wts_ref.shape[0]
    ds = sw2_ref.shape[0]
    sub = hin_ref.shape[1] // y_hbm.shape[1]

    def row_copy(pos_ref, k, r, s):
        p = pl.multiple_of(pos_ref[0, 0, k * tm + r] * sub, sub)
        return pltpu.make_async_copy(y_hbm.at[pl.ds(p, sub)], ybuf.at[s, k, pl.ds(r * sub, sub)], sem.at[s])

    def start_gather(pos_ref, s):
        for k in range(K):
            for r in range(tm):
                row_copy(pos_ref, k, r, s).start(priority=r % 2)

    @pl.when(j == 0)
    def _():
        start_gather(pos_cur_ref, 0)

    @pl.when(j + 1 < nt)
    def _():
        start_gather(pos_nxt_ref, 1 - slot)

    for k in range(K):
        pltpu.make_async_copy(y_hbm.at[pl.ds(0, tm * sub)], ybuf.at[slot, k], sem.at[slot]).wait()

    def routed(k):
        rows = ybuf.at[slot, k]
        return jnp.concatenate([rows[pl.ds(i, tm, stride=sub), :] for i in range(sub)], axis=1)

    w = wts_ref[...]
    acc = routed(0) * w[:, 0:1]
    for k in range(1, K):
        acc = acc + routed(k) * w[:, k:k + 1]
    hb = hin_ref[...].astype(jnp.bfloat16)
    s13 = jnp.dot(hb, sw13_ref[...], preferred_element_type=jnp.float32)
    sh = (jax.nn.silu(s13[:, :ds]) * s13[:, ds:]).astype(jnp.bfloat16)
    acc = acc + jnp.dot(sh, sw2_ref[...], preferred_element_type=jnp.float32)
    h = hres_ref[...] + gt_ref[0] * acc
    ms = jnp.mean(h * h, axis=-1, keepdims=True)
    out_ref[...] = h * lax.rsqrt(ms + NORM_EPS) * fg_ref[...]


def moe_combine_norm(y_rows, pos, wts, h_in, h_res, gate, sw13, sw2, final_g, seq):
    N, D = h_in.shape
    K = pos.shape[1]
    lanes = y_rows.shape[1]
    tm = COMBINE_ROWS
    nt = N // tm
    tiles_per_seq = seq // tm
    pos3 = pos.reshape(nt, tm, K).transpose(0, 2, 1).reshape(nt, 1, K * tm)
    last = nt - 1
    return pl.pallas_call(
        _combine_body,
        grid=(nt,),
        in_specs=[
            pl.BlockSpec((1, 1, K * tm), lambda j: (j, 0, 0), memory_space=pltpu.SMEM),
            pl.BlockSpec((1, 1, K * tm), lambda j: (jnp.minimum(j + 1, last), 0, 0), memory_space=pltpu.SMEM),
            pl.BlockSpec(memory_space=pl.ANY),
            pl.BlockSpec((tm, K), lambda j: (j, 0)),
            pl.BlockSpec((tm, D), lambda j: (j, 0)),
            pl.BlockSpec((tm, D), lambda j: (j, 0)),
            pl.BlockSpec((1, 1, D), lambda j: (j // tiles_per_seq, 0, 0)),
            pl.BlockSpec(sw13.shape, lambda j: (0, 0)),
            pl.BlockSpec(sw2.shape, lambda j: (0, 0)),
            pl.BlockSpec((1, D), lambda j: (0, 0)),
        ],
        out_specs=pl.BlockSpec((tm, D), lambda j: (j, 0)),
        out_shape=jax.ShapeDtypeStruct((N, D), jnp.float32),
        scratch_shapes=[pltpu.VMEM((2, K, tm * (D // lanes), lanes), jnp.float32), pltpu.SemaphoreType.DMA((2,))],
        compiler_params=pltpu.CompilerParams(dimension_semantics=("arbitrary",), vmem_limit_bytes=VMEM_LIMIT),
    )(pos3, pos3, y_rows, wts, h_in, h_res, gate.reshape(-1, 1, D), sw13, sw2, final_g.reshape(1, D))


def moe_block_final(h_in, h_slab, h_res, gate, top_e, wts, w1, w3, w2, sw1, sw3, sw2, final_g, seq):
    bf = jnp.bfloat16
    blk_e, n_valid_blk, row_tok, pos = moe_layout(top_e, w1.shape[0])
    w13 = jnp.concatenate([w1, w3], axis=-1).astype(bf)
    y_rows = expert_ffn_rows(h_slab, blk_e, n_valid_blk, row_tok, w13, w2.astype(bf))
    sw13 = jnp.concatenate([sw1, sw3], axis=-1).astype(bf)
    return moe_combine_norm(y_rows, pos, wts, h_in, h_res, gate, sw13, sw2.astype(bf), final_g, seq)


def kernel(x, c, ctx, c_ctx, w_ada, b_ada, norm1_g, w_in, mu_shift, w0, w_up, a0, a_up, g_up,
           k_k, k_a, r_k, lnx_w, lnx_b, conv_w, A_log, dt_bias, onorm_g, w_out, norm2_g,
           router_w, router_b, exp_w1, exp_w3, exp_w2, sh_w1, sh_w3, sh_w2, final_g):
    h_x, h_c = x, ctx
    l = 0
    mod = jax.nn.silu(c) @ w_ada[l] + b_ada[l]
    mod_c = jax.nn.silu(c_ctx) @ w_ada[l] + b_ada[l]
    sh1, sc1, gt1, sh2, sc2, gt2 = (m[:, None] for m in jnp.split(mod, 6, axis=-1))
    sh1c, sc1c, gt1c, sh2c, sc2c, gt2c = jnp.split(mod_c, 6, axis=-1)

    hm_x = rmsnorm(h_x, norm1_g[l]) * (1.0 + sc1) + sh1
    hm_c = rmsnorm(h_c, norm1_g[l]) * (1.0 + sc1c) + sh1c
    w_in_a, w_in_b = w_in[l][:, :RWKV_COLS], w_in[l][:, RWKV_COLS:]
    ya_x = rwkv7_mixer_latent(hm_c @ w_in_a, hm_x @ w_in_a, mu_shift[l], w0[l], w_up[l],
                              a0[l], a_up[l], g_up[l], k_k[l], k_a[l], r_k[l], lnx_w[l], lnx_b[l])
    yb_x = gdn_mixer_latent(hm_c @ w_in_b, hm_x @ w_in_b, conv_w[l], A_log[l], dt_bias[l], onorm_g[l])
    B, T, D = h_x.shape
    N = B * T
    h_res, h_in, h_slab, top_e, wts = mixer_out_route(
        ya_x.reshape(N, DA), yb_x.reshape(N, DB), x.reshape(N, D), gt1[:, 0], sc2[:, 0], sh2[:, 0],
        w_out[l], norm2_g[l], router_w[l], router_b[l], T)
    out = moe_block_final(h_in, h_slab, h_res, gt2[:, 0], top_e.T, wts.T,
                          exp_w1[l], exp_w3[l], exp_w2[l], sh_w1[l], sh_w3[l], sh_w2[l], final_g, T)
    return out.reshape(B, T, D)
```

```python
import functools

import jax
import jax.numpy as jnp
from jax import lax
from jax.experimental import pallas as pl
from jax.experimental.pallas import tpu as pltpu

D_MODEL = 1024
BATCH = 32
SEQ = 2048
DEPTH = 1
CTX_LEN = 256
GRID_W = 64
D_MIX = D_MODEL
DA = D_MIX // 2
NA = 64
HA = DA // NA
DECAY_LORA = 64
AAA_LORA = 64
GATE_LORA = 128
RWKV_COLS = 3 * DA + 2 * DECAY_LORA + 2 * AAA_LORA + GATE_LORA
RWKV_EPS = 64e-5
DB = D_MIX - DA
HB = 4
KB = DB // HB
GDN_CONV = 5
GDN_CHUNK = 64
GDN_COLS = 4 * DB + 4 * HB
IN_COLS = RWKV_COLS + GDN_COLS
N_EXPERTS = 256
TOP_K = 8
N_GROUPS = 8
TOPK_GROUPS = 4
D_EXPERT = 256
D_SHARED = 256
ROUTED_SCALE = 2.5
MOE_BLOCK = 256
NORM_EPS = 1e-6


def rmsnorm(x, g):
    xf = x.astype(jnp.float32)
    y = xf * lax.rsqrt(jnp.mean(xf * xf, axis=-1, keepdims=True) + NORM_EPS)
    return (y * g.astype(jnp.float32)).astype(x.dtype)


def l2norm(x):
    xf = x.astype(jnp.float32)
    return xf * lax.rsqrt(jnp.sum(xf * xf, axis=-1, keepdims=True) + 1e-6)


def shift_seq(p):
    h = p.shape[-1] // 2
    prev = jnp.pad(p[:, :-1, :h], ((0, 0), (1, 0), (0, 0)))
    nxt = jnp.pad(p[:, 1:, h:], ((0, 0), (0, 1), (0, 0)))
    return jnp.concatenate([prev, nxt], axis=-1)


def shift_grid(p):
    B, T, C = p.shape
    rows = T // GRID_W
    q = C // 4
    g = p.reshape(B, rows, GRID_W, C)
    left = jnp.pad(g[:, :, :-1, :q], ((0, 0), (0, 0), (1, 0), (0, 0)))
    right = jnp.pad(g[:, :, 1:, q:2 * q], ((0, 0), (0, 0), (0, 1), (0, 0)))
    up = jnp.pad(g[:, :-1, :, 2 * q:3 * q], ((0, 0), (1, 0), (0, 0), (0, 0)))
    down = jnp.pad(g[:, 1:, :, 3 * q:], ((0, 0), (0, 1), (0, 0), (0, 0)))
    return jnp.concatenate([left, right, up, down], axis=-1).reshape(B, T, C)


def dwconv_centered(x, w):
    C = x.shape[-1]
    return lax.conv_general_dilated(
        x, w[:, None, :].astype(x.dtype), window_strides=(1,),
        padding=[(GDN_CONV // 2, GDN_CONV // 2)],
        dimension_numbers=('NWC', 'WIO', 'NWC'), feature_group_count=C)


def gdn_prepare(p, conv_w, A_log, dt_bias):
    B, T, _ = p.shape
    qkv = jax.nn.silu(dwconv_centered(p[..., :3 * DB], conv_w)).astype(jnp.float32)
    q, k, v = jnp.split(qkv, 3, axis=-1)
    q = l2norm(q.reshape(B, T, HB, KB)) * (KB ** -0.5)
    k = l2norm(k.reshape(B, T, HB, KB))
    v = v.reshape(B, T, HB, KB)
    z = p[..., 3 * DB:4 * DB]
    gl = p[..., 4 * DB:].astype(jnp.float32).reshape(B, T, 2, 2, HB)
    g = -jnp.exp(A_log) * jax.nn.softplus(gl[:, :, 0] + dt_bias)
    beta = jax.nn.sigmoid(gl[:, :, 1])
    return q, k, v, g, beta, z


WKV_CHUNK = 64
WKV_GROUP = 4
WKV_WIDTH = WKV_GROUP * NA
WKV_ROWS = 2


def _wkv_chunk_body(r_ref, v_ref, kk_ref, lw_ref, k_ref, b_ref, y_ref, st_ref):
    W = WKV_WIDTH
    fwd = pl.program_id(0) == 0

    @pl.when(pl.program_id(2) == 0)
    def _():
        st_ref[...] = jnp.zeros_like(st_ref)

    rows, n_groups = r_ref.shape[0], r_ref.shape[2] // W
    where = [(i, slice(g * W, (g + 1) * W)) for i in range(rows) for g in range(n_groups)]
    chains = [_wkv_chunk_math(fwd, r_ref[i, :, lanes], v_ref[i, :, lanes], kk_ref[i, :, lanes],
                              lw_ref[0, i, :, lanes], k_ref[0, i, :, lanes], b_ref[0, i, :, lanes], st_ref[s])
              for s, (i, lanes) in enumerate(where)]
    for s, (y, st) in enumerate(_interleave(chains)):
        i, lanes = where[s]
        y_ref[0, i, :, lanes] = y
        st_ref[s] = st


def _interleave(chains):
    results = [None] * len(chains)
    live = list(range(len(chains)))
    while live:
        for i in list(live):
            try:
                next(chains[i])
            except StopIteration as stop:
                results[i] = stop.value
                live.remove(i)
    return results


def _wkv_chunk_math(fwd, r, v, kk, lw, k, b, st):
    C, W, G = WKV_CHUNK, WKV_WIDTH, WKV_GROUP
    f32, bf16 = jnp.float32, jnp.bfloat16
    sgn = jnp.where(fwd, 1, -1)

    row = lax.broadcasted_iota(jnp.int32, (C, C), 0)
    col = lax.broadcasted_iota(jnp.int32, (C, C), 1)
    tri = jnp.where((col - row) * sgn <= 0, 1.0, 0.0).astype(bf16)
    hi = lw.astype(bf16)
    rem = lw - hi.astype(f32)
    mid = rem.astype(bf16)
    lo = (rem - mid.astype(f32)).astype(bf16)
    dot = functools.partial(jnp.dot, preferred_element_type=f32)
    cinc = dot(tri, hi) + dot(tri, mid) + dot(tri, lo)
    yield
    ctot = jnp.where(fwd, cinc[C - 1:C], cinc[0:1])
    e_out = jnp.exp(-cinc)
    e_last = jnp.exp(ctot - cinc)

    rr = lax.broadcasted_iota(jnp.int32, (G * C, W), 0)
    cc = lax.broadcasted_iota(jnp.int32, (G * C, W), 1)
    same_head = (rr // C) == (cc // NA)

    def bdiag(x):
        return jnp.where(same_head, jnp.concatenate([x] * G, axis=0), 0.0).astype(bf16)

    lhs = jnp.concatenate([bdiag(kk * jnp.exp(cinc - lw)), bdiag(r * jnp.exp(cinc))], axis=0)
    rhs = jnp.concatenate([bdiag(b * e_out), bdiag(k * e_out)], axis=0)
    trans_b = (((1,), (1,)), ((), ()))
    gram = lax.dot_general(lhs, rhs, trans_b, preferred_element_type=f32)
    hs = lax.dot_general(lhs, st.astype(bf16), trans_b, preferred_element_type=f32)
    yield

    tr, tc = rr % C, cc % C
    order = (tc - tr) * sgn
    strict = order < 0
    incl = order <= 0
    n = G * C
    a_bk = jnp.where(strict, gram[:n, :n], 0.0)
    a_kk = jnp.where(strict, gram[:n, n:], 0.0)
    rb = jnp.where(incl, gram[n:, :n], 0.0)
    rk = jnp.where(incl, gram[n:, n:], 0.0)

    vbd = bdiag(v)
    x = hs[:n] + dot(a_kk.astype(bf16), vbd)
    p = -a_bk
    steps = C.bit_length() - 1
    for i in range(steps):
        yield
        pb = p.astype(bf16)
        x = x + dot(pb, x.astype(bf16))
        if i + 1 < steps:
            p = dot(pb, pb)
    ub = x.astype(bf16)
    yield

    y = hs[n:] + dot(jnp.concatenate([rk, -rb], axis=1).astype(bf16), jnp.concatenate([vbd, ub], axis=0))
    yield

    trans_a = (((0,), (0,)), ((), ()))
    upd = lax.dot_general(jnp.concatenate([vbd, -ub], axis=0),
                          jnp.concatenate([bdiag(k * e_last), bdiag(b * e_last)], axis=0),
                          trans_a, preferred_element_type=f32)
    return sum(y[h * C:(h + 1) * C] for h in range(G)), st * jnp.exp(ctot) + upd


def wkv7_chunked(r, v, kk, lw, k_dir, b_dir, n_ctx_chunks):
    B, T, da = r.shape
    C, W = WKV_CHUNK, WKV_WIDTH
    n = T // C

    def chunk(d, c):
        back = jnp.where(c < n_ctx_chunks, n_ctx_chunks - 1 - c, n + n_ctx_chunks - 1 - c)
        return jnp.where(d == 0, c, back)

    rows = WKV_ROWS
    shared = pl.BlockSpec((rows, C, da), lambda d, bi, c: (bi, chunk(d, c), 0))
    per_dir = pl.BlockSpec((1, rows, C, da), lambda d, bi, c: (d, bi, chunk(d, c), 0))
    return pl.pallas_call(
        _wkv_chunk_body,
        grid=(2, B // rows, n),
        in_specs=[shared, shared, shared, per_dir, per_dir, per_dir],
        out_specs=per_dir,
        out_shape=jax.ShapeDtypeStruct((2, B, T, da), jnp.float32),
        scratch_shapes=[pltpu.VMEM((rows * (da // W), W, W), jnp.float32)],
        compiler_params=pltpu.CompilerParams(dimension_semantics=("arbitrary",) * 3,
                                             vmem_limit_bytes=VMEM_LIMIT),
    )(r, v, kk, lw, k_dir, b_dir)


def rwkv7_prepare_bm(p, shifted, mu, w0, w_up, a0, a_up, g_up, k_k, k_a):
    p = (p + mu * (shifted - p)).astype(jnp.float32)
    B, T = p.shape[:2]
    cuts = [DA, 2 * DA, 3 * DA, 3 * DA + 2 * DECAY_LORA, 3 * DA + 2 * DECAY_LORA + 2 * AAA_LORA]
    r, k, v, wl, al, gl = jnp.split(p, cuts, axis=-1)
    wl = wl.reshape(B, T, 2, DECAY_LORA)
    al = al.reshape(B, T, 2, AAA_LORA)
    w_log = -jax.nn.softplus(-(w0[:, None, None] + jnp.einsum('btdl,dlc->dbtc', jnp.tanh(wl), w_up))) - 0.5
    lw = -jnp.exp(w_log)
    a = jax.nn.sigmoid(a0[:, None, None] + jnp.einsum('btdl,dlc->dbtc', al, a_up))
    g = jax.nn.sigmoid(gl) @ g_up
    kk = l2norm((k * k_k).reshape(B, T, HA, NA)).reshape(B, T, DA)
    k_dir = k[None] * (1.0 + (a - 1.0) * k_a)
    b_dir = kk[None] * a
    return r, v, kk, g, lw, k_dir, b_dir


def rwkv7_mixer_latent(p_ctx, p_lat, mu, w0, w_up, a0, a_up, g_up, k_k, k_a, r_k, lnx_w, lnx_b):
    t_ctx = p_ctx.shape[1]
    p = jnp.concatenate([p_ctx, p_lat], axis=1)
    shifted = jnp.concatenate([shift_seq(p_ctx), shift_grid(p_lat)], axis=1)
    r, v, kk, g, lw, k_dir, b_dir = rwkv7_prepare_bm(p, shifted, mu, w0, w_up, a0, a_up, g_up, k_k, k_a)
    y2 = wkv7_chunked(r, v, kk, lw, k_dir, b_dir, t_ctx // WKV_CHUNK)
    sl = lambda t: t[..., t_ctx:, :]
    r, v, g, k_dir = sl(r), sl(v), sl(g), sl(k_dir)
    B, T = r.shape[:2]
    y = (sl(y2[0]) + sl(y2[1])).reshape(B, T, HA, NA)
    m = jnp.mean(y, axis=-1, keepdims=True)
    var = jnp.mean(jnp.square(y - m), axis=-1, keepdims=True)
    y = ((y - m) * lax.rsqrt(var + RWKV_EPS)).reshape(B, T, DA) * lnx_w + lnx_b
    heads = lambda t: t.reshape(t.shape[:-1] + (HA, NA))
    bonus = jnp.sum(heads(r)[None] * heads(k_dir) * r_k, axis=(0, 4))
    return (y + (bonus[..., None] * heads(v)).reshape(B, T, DA)) * g


GDN_ROWS = 4


def _gdn_chunk_body(q_ref, k_ref, v_ref, gc_ref, grow_ref, beta_ref, o_ref, st_ref):
    fwd = pl.program_id(0) == 0

    @pl.when(pl.program_id(2) == 0)
    def _():
        st_ref[...] = jnp.zeros_like(st_ref)

    def over_lanes(a):
        return jnp.concatenate([jnp.broadcast_to(a[:, h:h + 1], (a.shape[0], KB)) for h in range(HB)], axis=1)

    chains = [_gdn_chunk_math(fwd, q_ref[i], k_ref[i], v_ref[i], over_lanes(gc_ref[0, i]), grow_ref[0, i, 0],
                              over_lanes(beta_ref[0, i]), st_ref[i]) for i in range(q_ref.shape[0])]
    for i, (o, st) in enumerate(_interleave(chains)):
        o_ref[0, i] = o
        st_ref[i] = st


def _gdn_chunk_math(fwd, q, k, v, gcx, grow, bx, st):
    C, H, KD = GDN_CHUNK, HB, KB
    W, n = H * KD, H * C
    f32, bf16 = jnp.float32, jnp.bfloat16
    sgn = jnp.where(fwd, 1, -1)
    dot = functools.partial(jnp.dot, preferred_element_type=f32)
    head = lambda x, h: x[:, h * KD:(h + 1) * KD]

    rr = lax.broadcasted_iota(jnp.int32, (n, W), 0)
    cc = lax.broadcasted_iota(jnp.int32, (n, W), 1)
    same_head = (rr // C) == (cc // KD)

    def bdiag(x):
        return jnp.where(same_head, jnp.concatenate([x] * H, axis=0), 0.0).astype(bf16)

    eg = jnp.exp(gcx)
    kb = k * bx
    trans_b = (((1,), (1,)), ((), ()))
    gram = lax.dot_general(jnp.concatenate([bdiag(kb), bdiag(q)], axis=0), bdiag(k), trans_b,
                           preferred_element_type=f32)
    yield

    r2 = lax.broadcasted_iota(jnp.int32, (n, n), 0)
    c2 = lax.broadcasted_iota(jnp.int32, (n, n), 1)
    order = (c2 % C - r2 % C) * sgn
    same2 = (r2 // C) == (c2 // C)
    incl = same2 & (order <= 0)
    strict = same2 & (order < 0)
    g_rows = jnp.concatenate([jnp.concatenate([head(gcx, h)] * (n // KD), axis=1) for h in range(H)], axis=0)
    decay = jnp.exp(jnp.where(incl, g_rows - grow, 0.0))
    a_low = jnp.where(strict, gram[:n] * decay, 0.0)
    attn = jnp.where(incl, gram[n:] * decay, 0.0)

    vb, kbe = v * bx, kb * eg
    x = jnp.concatenate([jnp.concatenate([head(vb, h), head(kbe, h)], axis=1) for h in range(H)], axis=0)
    p = -a_low
    steps = C.bit_length() - 1
    for i in range(steps):
        yield
        pb = p.astype(bf16)
        x = x + dot(pb, x.astype(bf16))
        if i + 1 < steps:
            p = dot(pb, pb)
    u, w = x[:, :KD], x[:, KD:]
    yield

    stb = st.astype(bf16)
    w_bd = jnp.where(same_head, jnp.concatenate([w] * H, axis=1), 0.0).astype(bf16)
    v_new = (u - dot(w_bd, stb)).astype(bf16)
    yield
    o = dot(jnp.concatenate([bdiag(q * eg), attn.astype(bf16)], axis=1), jnp.concatenate([stb, v_new], axis=0))
    yield

    g_last = jnp.where(fwd, gcx[C - 1:C], gcx[0:1])
    trans_a = (((0,), (0,)), ((), ()))
    upd = lax.dot_general(bdiag(k * jnp.exp(g_last - gcx)), v_new, trans_a, preferred_element_type=f32)
    e_last = jnp.exp(g_last)
    st_new = jnp.concatenate([st[h * KD:(h + 1) * KD] * head(e_last, h) for h in range(H)], axis=0) + upd
    return jnp.concatenate([o[h * C:(h + 1) * C] for h in range(H)], axis=1), st_new


def gdn_chunked_pallas(q, k, v, gc, grow, beta, n_ctx_chunks):
    B, T, W = q.shape
    C = GDN_CHUNK
    n = T // C

    def chunk(d, c):
        back = jnp.where(c < n_ctx_chunks, n_ctx_chunks - 1 - c, n + n_ctx_chunks - 1 - c)
        return jnp.where(d == 0, c, back)

    nb = GDN_ROWS
    shared = pl.BlockSpec((nb, C, W), lambda d, bi, c: (bi, chunk(d, c), 0))
    per_dir = pl.BlockSpec((1, nb, C, W), lambda d, bi, c: (d, bi, chunk(d, c), 0))
    rows = pl.BlockSpec((1, nb, 1, 1, HB * C), lambda d, bi, c: (d, bi, chunk(d, c), 0, 0))
    per_head = pl.BlockSpec((1, nb, C, HB), lambda d, bi, c: (d, bi, chunk(d, c), 0))
    return pl.pallas_call(
        _gdn_chunk_body,
        grid=(2, B // nb, n),
        in_specs=[shared, shared, shared, per_head, rows, per_head],
        out_specs=per_dir,
        out_shape=jax.ShapeDtypeStruct((2, B, T, W), jnp.float32),
        scratch_shapes=[pltpu.VMEM((nb, W, KB), jnp.float32)],
        compiler_params=pltpu.CompilerParams(dimension_semantics=("arbitrary",) * 3,
                                             vmem_limit_bytes=VMEM_LIMIT),
    )(q, k, v, gc, grow, beta)


def gdn_mixer_latent(p_ctx, p_lat, conv_w, A_log, dt_bias, onorm_g):
    t_ctx = p_ctx.shape[1]
    qc, kc, vc, gc_, bc, _ = gdn_prepare(p_ctx, conv_w, A_log, dt_bias)
    ql, kl, vl, gl_, bl, z = gdn_prepare(p_lat, conv_w, A_log, dt_bias)
    cat = lambda a, b: jnp.concatenate([a, b], axis=1)
    B, T = ql.shape[0], t_ctx + ql.shape[1]
    C = GDN_CHUNK
    flat = lambda t: t.reshape(B, T, DB)
    q, k, v = flat(cat(qc, ql)), flat(cat(kc, kl)), flat(cat(vc, vl))
    g = jnp.moveaxis(cat(gc_, gl_), 2, 0).reshape(2, B, T // C, C, HB)
    gcs = jnp.stack([jnp.cumsum(g[0], axis=2), lax.cumsum(g[1], axis=2, reverse=True)])
    grow = jnp.swapaxes(gcs, 3, 4).reshape(2, B, T // C, 1, HB * C)
    beta = jnp.moveaxis(cat(bc, bl), 2, 0)
    o2 = gdn_chunked_pallas(q, k, v, gcs.reshape(2, B, T, HB), grow, beta, t_ctx // C)
    o = (o2[0, :, t_ctx:] + o2[1, :, t_ctx:]).reshape(B, T - t_ctx, HB, KB)
    o = o * lax.rsqrt(jnp.mean(o * o, axis=-1, keepdims=True) + NORM_EPS) * onorm_g
    return o.reshape(B, T - t_ctx, DB) * jax.nn.silu(z.astype(jnp.float32))


LANES = 128
ROW_BLOCK = 256
COMBINE_ROWS = 128
LAYOUT_TOKENS = 256
VMEM_LIMIT = 48 * 1024 * 1024


def moe_route(xs, router_w, router_b):
    N = xs.shape[0]
    E = router_w.shape[1]
    scores = jax.nn.sigmoid((xs @ router_w).astype(jnp.float32))
    sel = scores + router_b.astype(jnp.float32)
    grp_score = jnp.sum(lax.top_k(sel.reshape(N, N_GROUPS, E // N_GROUPS), 2)[0], axis=-1)
    top_g = lax.top_k(grp_score, TOPK_GROUPS)[1]
    gmask = jnp.any(top_g[:, :, None] == jnp.arange(N_GROUPS)[None, None, :], axis=1)
    emask = jnp.repeat(gmask, E // N_GROUPS, axis=1)
    top_e = lax.top_k(jnp.where(emask, sel, -jnp.inf), TOP_K)[1]
    wts = jnp.take_along_axis(scores, top_e, axis=1)
    wts = wts / jnp.sum(wts, axis=-1, keepdims=True) * ROUTED_SCALE
    return top_e, wts


MID_ROWS = 256


def _first_max(x, idx, axis_len):
    m = jnp.max(x, axis=0, keepdims=True)
    i = jnp.min(jnp.where(x == m, idx, axis_len), axis=0, keepdims=True)
    return m, i


def _mid_body(ya_ref, yb_ref, x_ref, gt1_ref, sc2_ref, sh2_ref, wo_ref, g2_ref, rwt_ref, rb_ref,
              hx_ref, hin_ref, slab_ref, te_ref, wt_ref):
    f32, bf16 = jnp.float32, jnp.bfloat16
    half = ya_ref.shape[1]
    mix = (jnp.dot(ya_ref[...].astype(bf16), wo_ref[:half], preferred_element_type=f32)
           + jnp.dot(yb_ref[...].astype(bf16), wo_ref[half:], preferred_element_type=f32))
    hx = x_ref[...] + gt1_ref[0] * mix
    hx_ref[...] = hx
    ms = jnp.mean(hx * hx, axis=-1, keepdims=True)
    hin = hx * lax.rsqrt(ms + NORM_EPS) * g2_ref[...] * (1.0 + sc2_ref[0]) + sh2_ref[0]
    hin_ref[...] = hin
    lanes = slab_ref.shape[1]
    sub = hin.shape[1] // lanes
    for j in range(sub):
        slab_ref[pl.ds(j, hin.shape[0], stride=sub), :] = hin[:, j * lanes:(j + 1) * lanes]

    logits = lax.dot_general(rwt_ref[...], hin.astype(bf16), (((1,), (1,)), ((), ())),
                             preferred_element_type=f32)
    n_e, tm = logits.shape
    per_group = n_e // N_GROUPS
    scores = jax.nn.sigmoid(logits)
    sel = scores + rb_ref[...]
    neg = -jnp.inf

    lidx = lax.broadcasted_iota(jnp.int32, (per_group, tm), 0)
    gs = []
    for g in range(N_GROUPS):
        blk = sel[g * per_group:(g + 1) * per_group]
        m1, i1 = _first_max(blk, lidx, per_group)
        m2 = jnp.max(jnp.where(lidx == i1, neg, blk), axis=0, keepdims=True)
        gs.append(m1 + m2)
    gsc = jnp.concatenate(gs, axis=0)

    gidx = lax.broadcasted_iota(jnp.int32, (N_GROUPS, tm), 0)
    chosen = jnp.zeros((N_GROUPS, tm), f32)
    for _ in range(TOPK_GROUPS):
        _, gi = _first_max(gsc, gidx, N_GROUPS)
        hit = gidx == gi
        chosen = jnp.where(hit, 1.0, chosen)
        gsc = jnp.where(hit, neg, gsc)
    allowed = jnp.concatenate([jnp.broadcast_to(chosen[g:g + 1], (per_group, tm)) for g in range(N_GROUPS)], axis=0)
    masked = jnp.where(allowed > 0.5, sel, neg)

    eidx = lax.broadcasted_iota(jnp.int32, (n_e, tm), 0)
    ids, ws = [], []
    for _ in range(TOP_K):
        _, ei = _first_max(masked, eidx, n_e)
        hit = eidx == ei
        ids.append(ei)
        ws.append(jnp.sum(jnp.where(hit, scores, 0.0), axis=0, keepdims=True))
        masked = jnp.where(hit, neg, masked)
    w = jnp.concatenate(ws, axis=0)
    te_ref[...] = jnp.concatenate(ids, axis=0)
    wt_ref[...] = w / jnp.sum(w, axis=0, keepdims=True) * ROUTED_SCALE


def mixer_out_route(ya, yb, x, gt1, sc2, sh2, w_out, norm2_g, router_w, router_b, seq):
    N, D = x.shape
    sub = D // LANES
    E = router_w.shape[1]
    tm = MID_ROWS
    per_seq = seq // tm
    row = lambda i: (i, 0)
    per_batch = pl.BlockSpec((1, 1, D), lambda i: (i // per_seq, 0, 0))
    whole = lambda a: pl.BlockSpec(a.shape, lambda i: (0,) * a.ndim)
    wo = w_out.astype(jnp.bfloat16)
    rwt = router_w.T.astype(jnp.bfloat16)
    g2 = norm2_g.reshape(1, D)
    rb = router_b.reshape(E, 1).astype(jnp.float32)
    b3 = lambda a: a.reshape(-1, 1, D)
    return pl.pallas_call(
        _mid_body,
        grid=(N // tm,),
        in_specs=[pl.BlockSpec((tm, D // 2), row), pl.BlockSpec((tm, D // 2), row), pl.BlockSpec((tm, D), row),
                  per_batch, per_batch, per_batch, whole(wo), whole(g2), whole(rwt), whole(rb)],
        out_specs=[pl.BlockSpec((tm, D), row), pl.BlockSpec((tm, D), row), pl.BlockSpec((tm * sub, LANES), row),
                   pl.BlockSpec((TOP_K, tm), lambda i: (0, i)), pl.BlockSpec((TOP_K, tm), lambda i: (0, i))],
        out_shape=[jax.ShapeDtypeStruct((N, D), jnp.float32), jax.ShapeDtypeStruct((N, D), jnp.float32),
                   jax.ShapeDtypeStruct((N * sub, LANES), jnp.float32),
                   jax.ShapeDtypeStruct((TOP_K, N), jnp.int32), jax.ShapeDtypeStruct((TOP_K, N), jnp.float32)],
        compiler_params=pltpu.CompilerParams(dimension_semantics=("arbitrary",), vmem_limit_bytes=VMEM_LIMIT),
    )(ya, yb, x, b3(gt1), b3(sc2), b3(sh2), wo, g2, rwt, rb)


def moe_layout(top_e, n_experts):
    N, K = top_e.shape
    nk = N * K
    tb = LAYOUT_TOKENS
    eids = jnp.arange(n_experts, dtype=jnp.int32)
    hit = top_e[:, :, None] == eids[None, None, :]
    used = jnp.any(hit, axis=1).astype(jnp.bfloat16).reshape(N // tb, tb, n_experts)
    before = (jnp.arange(tb)[:, None] > jnp.arange(tb)[None, :]).astype(jnp.bfloat16)
    within = jnp.einsum('ij,bje->bie', before, used, preferred_element_type=jnp.float32)
    blk_tot = jnp.sum(used.astype(jnp.float32), axis=1)
    blk_before = jnp.cumsum(blk_tot, axis=0) - blk_tot
    rank = (within + blk_before[:, None, :]).reshape(N, n_experts).astype(jnp.int32)
    cnt = jnp.sum(blk_tot, axis=0).astype(jnp.int32)
    padded = (cnt + ROW_BLOCK - 1) // ROW_BLOCK * ROW_BLOCK
    pend = jnp.cumsum(padded).astype(jnp.int32)
    row0 = rank + (pend - padded)[None, :]
    pos = jnp.sum(jnp.where(hit, row0[:, None, :], 0), axis=2)
    n_blk = (nk + n_experts * (ROW_BLOCK - 1) + ROW_BLOCK - 1) // ROW_BLOCK
    tok = jnp.broadcast_to(jnp.arange(N, dtype=jnp.int32)[:, None], (N, K))
    row_tok = jnp.zeros((n_blk * ROW_BLOCK,), jnp.int32).at[pos.reshape(nk)].set(tok.reshape(nk))
    blk_start = jnp.arange(n_blk, dtype=jnp.int32) * ROW_BLOCK
    blk_e = jnp.minimum(jnp.searchsorted(pend, blk_start, side='right'), n_experts - 1).astype(jnp.int32)
    n_valid_blk = (pend[-1] // ROW_BLOCK).astype(jnp.int32).reshape(1)
    return blk_e, n_valid_blk, row_tok, pos


def _expert_ffn_body(blk_e_ref, nvb_ref, tok_cur_ref, tok_nxt_ref, x_hbm, w13_ref, w2_ref, y_ref, xbuf, sem):
    i = pl.program_id(0)
    nvb = nvb_ref[0]
    slot = lax.rem(i, 2)
    de = w2_ref.shape[1]
    sub = w13_ref.shape[1] // x_hbm.shape[1]

    def row_copy(tok_ref, r, s):
        t = pl.multiple_of(tok_ref[0, 0, r] * sub, sub)
        return pltpu.make_async_copy(x_hbm.at[pl.ds(t, sub)], xbuf.at[s, pl.ds(r * sub, sub)], sem.at[s])

    def start_gather(tok_ref, s, first=0, count=ROW_BLOCK):
        for r in range(first, first + count):
            row_copy(tok_ref, r, s).start(priority=r % 2)

    @pl.when((i == 0) & (nvb > 0))
    def _():
        start_gather(tok_cur_ref, 0)

    @pl.when(i < nvb)
    def _():
        pltpu.make_async_copy(x_hbm.at[pl.ds(0, ROW_BLOCK * sub)], xbuf.at[slot], sem.at[slot]).wait()
        per_chunk = min(2, sub)
        n_chunks = sub // per_chunk
        width = per_chunk * x_hbm.shape[1]
        h13 = None
        for c in range(n_chunks):
            @pl.when(i + 1 < nvb)
            def _():
                start_gather(tok_nxt_ref, 1 - slot, c * (ROW_BLOCK // n_chunks), ROW_BLOCK // n_chunks)
            xc = jnp.concatenate([xbuf[slot, pl.ds(j, ROW_BLOCK, stride=sub), :]
                                  for j in range(c * per_chunk, (c + 1) * per_chunk)], axis=1).astype(jnp.bfloat16)
            part = jnp.dot(xc, w13_ref[0, c * width:(c + 1) * width, :], preferred_element_type=jnp.float32)
            h13 = part if h13 is None else h13 + part
        h = (jax.nn.silu(h13[:, :de]) * h13[:, de:]).astype(jnp.bfloat16)
        y = jnp.dot(h, w2_ref[0], preferred_element_type=jnp.float32)
        lanes = y_ref.shape[1]
        for j in range(sub):
            y_ref[pl.ds(j, ROW_BLOCK, stride=sub), :] = y[:, j * lanes:(j + 1) * lanes]

    @pl.when(i >= nvb)
    def _():
        y_ref[...] = jnp.zeros_like(y_ref)


def expert_ffn_rows(xs, blk_e, n_valid_blk, row_tok, w13, w2):
    D = w13.shape[1]
    lanes = xs.shape[1]
    n_blk = blk_e.shape[0]
    de2 = w13.shape[2]
    tok3 = row_tok.reshape(n_blk, 1, ROW_BLOCK)
    last = n_blk - 1
    grid_spec = pltpu.PrefetchScalarGridSpec(
        num_scalar_prefetch=2,
        grid=(n_blk,),
        in_specs=[
            pl.BlockSpec((1, 1, ROW_BLOCK), lambda i, be, nv: (i, 0, 0), memory_space=pltpu.SMEM),
            pl.BlockSpec((1, 1, ROW_BLOCK), lambda i, be, nv: (jnp.minimum(i + 1, last), 0, 0),
                         memory_space=pltpu.SMEM),
            pl.BlockSpec(memory_space=pl.ANY),
            pl.BlockSpec((1, D, de2), lambda i, be, nv: (be[i], 0, 0)),
            pl.BlockSpec((1, de2 // 2, D), lambda i, be, nv: (be[i], 0, 0)),
        ],
        out_specs=pl.BlockSpec((ROW_BLOCK * (D // lanes), lanes), lambda i, be, nv: (i, 0)),
        scratch_shapes=[pltpu.VMEM((2, ROW_BLOCK * (D // lanes), lanes), jnp.float32),
                        pltpu.SemaphoreType.DMA((2,))],
    )
    return pl.pallas_call(
        _expert_ffn_body,
        grid_spec=grid_spec,
        out_shape=jax.ShapeDtypeStruct((n_blk * ROW_BLOCK * (D // lanes), lanes), jnp.float32),
        compiler_params=pltpu.CompilerParams(dimension_semantics=("arbitrary",), vmem_limit_bytes=VMEM_LIMIT),
    )(blk_e, n_valid_blk, tok3, tok3, xs, w13, w2)


def _combine_body(pos_cur_ref, pos_nxt_ref, y_hbm, wts_ref, hin_ref, hres_ref, gt_ref, sw13_ref, sw2_ref,
                  fg_ref, out_ref, ybuf, sem):
    j = pl.program_id(0)
    nt = pl.num_programs(0)
    slot = lax.rem(j, 2)
    K = wts_ref.shape[1]
    tm = wts_ref.shape[0]
    ds = sw2_ref.shape[0]
    sub = hin_ref.shape[1] // y_hbm.shape[1]

    def row_copy(pos_ref, k, r, s):
        p = pl.multiple_of(pos_ref[0, 0, k * tm + r] * sub, sub)
        return pltpu.make_async_copy(y_hbm.at[pl.ds(p, sub)], ybuf.at[s, k, pl.ds(r * sub, sub)], sem.at[s])

    def start_gather(pos_ref, s):
        for k in range(K):
            for r in range(tm):
                row_copy(pos_ref, k, r, s).start(priority=r % 2)

    @pl.when(j == 0)
    def _():
        start_gather(pos_cur_ref, 0)

    @pl.when(j + 1 < nt)
    def _():
        start_gather(pos_nxt_ref, 1 - slot)

    for k in range(K):
        pltpu.make_async_copy(y_hbm.at[pl.ds(0, tm * sub)], ybuf.at[slot, k], sem.at[slot]).wait()

    def routed(k):
        rows = ybuf.at[slot, k]
        return jnp.concatenate([rows[pl.ds(i, tm, stride=sub), :] for i in range(sub)], axis=1)

    w = wts_ref[...]
    acc = routed(0) * w[:, 0:1]
    for k in range(1, K):
        acc = acc + routed(k) * w[:, k:k + 1]
    hb = hin_ref[...].astype(jnp.bfloat16)
    s13 = jnp.dot(hb, sw13_ref[...], preferred_element_type=jnp.float32)
    sh = (jax.nn.silu(s13[:, :ds]) * s13[:, ds:]).astype(jnp.bfloat16)
    acc = acc + jnp.dot(sh, sw2_ref[...], preferred_element_type=jnp.float32)
    h = hres_ref[...] + gt_ref[0] * acc
    ms = jnp.mean(h * h, axis=-1, keepdims=True)
    out_ref[...] = h * lax.rsqrt(ms + NORM_EPS) * fg_ref[...]


def moe_combine_norm(y_rows, pos, wts, h_in, h_res, gate, sw13, sw2, final_g, seq):
    N, D = h_in.shape
    K = pos.shape[1]
    lanes = y_rows.shape[1]
    tm = COMBINE_ROWS
    nt = N // tm
    tiles_per_seq = seq // tm
    pos3 = pos.reshape(nt, tm, K).transpose(0, 2, 1).reshape(nt, 1, K * tm)
    last = nt - 1
    return pl.pallas_call(
        _combine_body,
        grid=(nt,),
        in_specs=[
            pl.BlockSpec((1, 1, K * tm), lambda j: (j, 0, 0), memory_space=pltpu.SMEM),
            pl.BlockSpec((1, 1, K * tm), lambda j: (jnp.minimum(j + 1, last), 0, 0), memory_space=pltpu.SMEM),
            pl.BlockSpec(memory_space=pl.ANY),
            pl.BlockSpec((tm, K), lambda j: (j, 0)),
            pl.BlockSpec((tm, D), lambda j: (j, 0)),
            pl.BlockSpec((tm, D), lambda j: (j, 0)),
            pl.BlockSpec((1, 1, D), lambda j: (j // tiles_per_seq, 0, 0)),
            pl.BlockSpec(sw13.shape, lambda j: (0, 0)),
            pl.BlockSpec(sw2.shape, lambda j: (0, 0)),
            pl.BlockSpec((1, D), lambda j: (0, 0)),
        ],
        out_specs=pl.BlockSpec((tm, D), lambda j: (j, 0)),
        out_shape=jax.ShapeDtypeStruct((N, D), jnp.float32),
        scratch_shapes=[pltpu.VMEM((2, K, tm * (D // lanes), lanes), jnp.float32), pltpu.SemaphoreType.DMA((2,))],
        compiler_params=pltpu.CompilerParams(dimension_semantics=("arbitrary",), vmem_limit_bytes=VMEM_LIMIT),
    )(pos3, pos3, y_rows, wts, h_in, h_res, gate.reshape(-1, 1, D), sw13, sw2, final_g.reshape(1, D))


def moe_block_final(h_in, h_slab, h_res, gate, top_e, wts, w1, w3, w2, sw1, sw3, sw2, final_g, seq):
    bf = jnp.bfloat16
    blk_e, n_valid_blk, row_tok, pos = moe_layout(top_e, w1.shape[0])
    w13 = jnp.concatenate([w1, w3], axis=-1).astype(bf)
    y_rows = expert_ffn_rows(h_slab, blk_e, n_valid_blk, row_tok, w13, w2.astype(bf))
    sw13 = jnp.concatenate([sw1, sw3], axis=-1).astype(bf)
    return moe_combine_norm(y_rows, pos, wts, h_in, h_res, gate, sw13, sw2.astype(bf), final_g, seq)


def kernel(x, c, ctx, c_ctx, w_ada, b_ada, norm1_g, w_in, mu_shift, w0, w_up, a0, a_up, g_up,
           k_k, k_a, r_k, lnx_w, lnx_b, conv_w, A_log, dt_bias, onorm_g, w_out, norm2_g,
           router_w, router_b, exp_w1, exp_w3, exp_w2, sh_w1, sh_w3, sh_w2, final_g):
    h_x, h_c = x, ctx
    l = 0
    mod = jax.nn.silu(c) @ w_ada[l] + b_ada[l]
    mod_c = jax.nn.silu(c_ctx) @ w_ada[l] + b_ada[l]
    sh1, sc1, gt1, sh2, sc2, gt2 = (m[:, None] for m in jnp.split(mod, 6, axis=-1))
    sh1c, sc1c, gt1c, sh2c, sc2c, gt2c = jnp.split(mod_c, 6, axis=-1)

    hm_x = rmsnorm(h_x, norm1_g[l]) * (1.0 + sc1) + sh1
    hm_c = rmsnorm(h_c, norm1_g[l]) * (1.0 + sc1c) + sh1c
    w_in_a, w_in_b = w_in[l][:, :RWKV_COLS], w_in[l][:, RWKV_COLS:]
    ya_x = rwkv7_mixer_latent(hm_c @ w_in_a, hm_x @ w_in_a, mu_shift[l], w0[l], w_up[l],
                              a0[l], a_up[l], g_up[l], k_k[l], k_a[l], r_k[l], lnx_w[l], lnx_b[l])
    yb_x = gdn_mixer_latent(hm_c @ w_in_b, hm_x @ w_in_b, conv_w[l], A_log[l], dt_bias[l], onorm_g[l])
    B, T, D = h_x.shape
    N = B * T
    h_res, h_in, h_slab, top_e, wts = mixer_out_route(
        ya_x.reshape(N, DA), yb_x.reshape(N, DB), x.reshape(N, D), gt1[:, 0], sc2[:, 0], sh2[:, 0],
        w_out[l], norm2_g[l], router_w[l], router_b[l], T)
    out = moe_block_final(h_in, h_slab, h_res, gt2[:, 0], top_e.T, wts.T,
                          exp_w1[l], exp_w3[l], exp_w2[l], sh_w1[l], sh_w3[l], sh_w2[l], final_g, T)
    return out.reshape(B, T, D)
```
